```python
import math
import jax, jax.numpy as jnp
from jax import lax
import numpy as np

D_MODEL = 1024
BATCH = 1
SEQ = 16384
DEPTH = 4

HEAD_DIM = 64
A_HEADS = 4
A_CONFIGS = ((128, 1), (512, 4), (2048, 16))
B_HEADS = 4
B_KV_HEADS = 2
B_WINDOW = 128
ROPE_THETA = 150000.0
C_HEADS = 8
C_KV_HEADS = 2
CMP_BLOCK = 32
CMP_STRIDE = 16
CMP_HIDDEN = 256
SEL_BLOCK = 64
SEL_TOPK = 16
C_WINDOW = 512
BLOCK = 128
REL_BUCKETS = 32
REL_MAX_DIST = 2048
D_FF = 4 * D_MODEL
MIX_WIDTH = (A_HEADS + B_HEADS + C_HEADS) * HEAD_DIM
DN_ALPHA = (2 * DEPTH) ** 0.25
DN_BETA = (8 * DEPTH) ** -0.25
LN_EPS = 1e-5
NEG = -1e30
FORCE = 1e4

IN_WIDTHS = (
    A_HEADS * HEAD_DIM, A_HEADS * HEAD_DIM, A_HEADS * HEAD_DIM,
    B_HEADS * HEAD_DIM, B_KV_HEADS * HEAD_DIM, B_KV_HEADS * HEAD_DIM,
    C_HEADS * HEAD_DIM,
    C_KV_HEADS * HEAD_DIM, C_KV_HEADS * HEAD_DIM,
    C_KV_HEADS * HEAD_DIM, C_KV_HEADS * HEAD_DIM,
    C_KV_HEADS * HEAD_DIM, C_KV_HEADS * HEAD_DIM,
    C_HEADS * 3,
)
N_IN = sum(IN_WIDTHS)
SPLIT_POINTS = tuple(int(v) for v in np.cumsum(IN_WIDTHS)[:-1])

kernel_name = "hybrid_dilated_swa_nsa_deepnorm_adaln"


def layer_norm(x, g, b):
    xf = x.astype(jnp.float32)
    mu = jnp.mean(xf, axis=-1, keepdims=True)
    var = jnp.mean(jnp.square(xf - mu), axis=-1, keepdims=True)
    return ((xf - mu) * lax.rsqrt(var + LN_EPS) * g + b).astype(x.dtype)


def t5_bucket(dist):
    exact = REL_BUCKETS // 2
    d = jnp.maximum(dist, 0)
    df = jnp.maximum(d, 1).astype(jnp.float32)
    large = exact + (jnp.log(df / exact) / math.log(REL_MAX_DIST / exact)
                     * (REL_BUCKETS - exact)).astype(jnp.int32)
    large = jnp.minimum(large, REL_BUCKETS - 1)
    return jnp.where(d < exact, d, large)


def rope(t, positions):
    half = t.shape[-1] // 2
    freq = ROPE_THETA ** (-jnp.arange(half, dtype=jnp.float32) / half)
    ang = positions[..., None].astype(jnp.float32) * freq
    cos, sin = jnp.cos(ang)[:, :, None, :], jnp.sin(ang)[:, :, None, :]
    t1, t2 = t[..., :half].astype(jnp.float32), t[..., half:].astype(jnp.float32)
    return jnp.concatenate([t1 * cos - t2 * sin, t2 * cos + t1 * sin], -1).astype(t.dtype)


def banded_attention(q, k, v, max_dist, rel_bias=None, dist_scale=1, sinks=None):
    bsz, L, H, hd = q.shape
    hkv = k.shape[2]
    g = H // hkv
    nblk = L // BLOCK
    nprev = -(-max_dist // BLOCK)
    kctx = (nprev + 1) * BLOCK
    qb = q.reshape(bsz, nblk, BLOCK, hkv, g, hd)

    def context(t):
        tp = jnp.pad(t, ((0, 0), (nprev * BLOCK, 0), (0, 0), (0, 0)))
        tb = tp.reshape(bsz, nblk + nprev, BLOCK, hkv, hd)
        return jnp.concatenate([tb[:, i:i + nblk] for i in range(nprev + 1)], axis=2)

    kc, vc = context(k), context(v)
    s = jnp.einsum('bnqhgd,bnkhd->bnhgqk', qb, kc).astype(jnp.float32) * (hd ** -0.5)
    rel = jnp.arange(kctx) - nprev * BLOCK
    dist = jnp.arange(BLOCK)[:, None] - rel[None, :]
    kpos = jnp.arange(nblk)[:, None, None] * BLOCK + rel[None, None, :]
    valid = (dist >= 0) & (dist <= max_dist) & (kpos >= 0)
    if rel_bias is not None:
        bias = rel_bias[t5_bucket(dist * dist_scale)]
        s = s + jnp.transpose(bias, (2, 0, 1)).reshape(hkv, g, BLOCK, kctx).astype(jnp.float32)
    s = jnp.where(valid[None, :, None, None], s, NEG)
    m = jnp.max(s, axis=-1, keepdims=True)
    if sinks is not None:
        sk = sinks.astype(jnp.float32).reshape(hkv, g)[None, None, :, :, None, None]
        m = jnp.maximum(m, sk)
    e = jnp.exp(s - m)
    den = jnp.sum(e, axis=-1, keepdims=True)
    if sinks is not None:
        den = den + jnp.exp(sk - m)
    p = (e / den).astype(v.dtype)
    out = jnp.einsum('bnhgqk,bnkhd->bnqhgd', p, vc).reshape(bsz, L, H, hd)
    lse = (m + jnp.log(den))[..., 0]
    return out, lse.transpose(0, 1, 4, 2, 3).reshape(bsz, L, H)


def dilated_attention(q, k, v, rel_bias):
    bsz, S, H, hd = q.shape
    outs, lses = [], []
    for window, dil in A_CONFIGS:
        span = dil * BLOCK
        L = -(-S // span) * span

        def to_sub(t):
            tp = jnp.pad(t, ((0, 0), (0, L - S), (0, 0), (0, 0)))
            return tp.reshape(bsz, L // dil, dil, H, hd).transpose(0, 2, 1, 3, 4).reshape(bsz * dil, L // dil, H, hd)

        o, lse = banded_attention(to_sub(q), to_sub(k), to_sub(v), window // dil,
                                  rel_bias=rel_bias, dist_scale=dil)
        outs.append(o.reshape(bsz, dil, L // dil, H, hd).transpose(0, 2, 1, 3, 4).reshape(bsz, L, H, hd)[:, :S])
        lses.append(lse.reshape(bsz, dil, L // dil, H).transpose(0, 2, 1, 3).reshape(bsz, L, H)[:, :S])
    w = jax.nn.softmax(jnp.stack(lses, axis=-1), axis=-1)
    return jnp.einsum('bshdc,bshc->bshd', jnp.stack(outs, axis=-1), w.astype(q.dtype))


def nsa_attention(q, k_cmp_tok, v_cmp_tok, k_slc, v_slc, k_win, v_win, gates,
                  cmp_pos, cmp_w1, cmp_w2, rel_bias):
    bsz, S, H, hd = q.shape
    hkv = k_slc.shape[2]
    g = H // hkv
    scale = hd ** -0.5
    r = CMP_BLOCK // CMP_STRIDE
    n_cmp = S // CMP_STRIDE - r + 1

    def compress(t, pos, w1, w2):
        tb = t.reshape(bsz, S // CMP_STRIDE, CMP_STRIDE, hkv, hd)
        blocks = jnp.concatenate([tb[:, i:i + n_cmp] for i in range(r)], axis=2)
        blocks = blocks + pos[None, None, :, None, :]
        flat = blocks.transpose(0, 1, 3, 2, 4).reshape(bsz, n_cmp, hkv, CMP_BLOCK * hd)
        return jax.nn.gelu(flat @ w1) @ w2

    kc = compress(k_cmp_tok, cmp_pos[0], cmp_w1[0], cmp_w2[0])
    vc = compress(v_cmp_tok, cmp_pos[1], cmp_w1[1], cmp_w2[1])

    n_sel = S // SEL_BLOCK
    topk = min(SEL_TOPK, n_sel)
    ks = k_slc.reshape(bsz, n_sel, SEL_BLOCK, hkv, hd).transpose(0, 3, 1, 2, 4)
    vs = v_slc.reshape(bsz, n_sel, SEL_BLOCK, hkv, hd).transpose(0, 3, 1, 2, 4)

    ratio = SEL_BLOCK // CMP_STRIDE
    front = r - 1
    offs = np.arange(-front, ratio)
    ov_w = jnp.asarray(np.array([max(0, min(SEL_BLOCK, o * CMP_STRIDE + CMP_BLOCK) - max(0, o * CMP_STRIDE))
                                 / CMP_BLOCK for o in offs], dtype=np.float32))
    map_idx = np.arange(n_sel)[:, None] * ratio + offs[None, :] + front
    pad_end = max(0, int(map_idx.max()) + 1 - (n_cmp + front))

    bias_c = rel_bias.reshape(REL_BUCKETS, hkv, g)
    cmp_end = jnp.arange(n_cmp) * CMP_STRIDE + CMP_BLOCK - 1
    sel_start = jnp.arange(n_sel) * SEL_BLOCK
    jj = jnp.arange(n_sel)
    bi = jnp.arange(bsz)[:, None, None, None]
    hi = jnp.arange(hkv)[None, :, None, None]
    hi5 = jnp.arange(hkv)[None, :, None, None, None]
    nblk = S // BLOCK
    qb = q.reshape(bsz, nblk, BLOCK, hkv, g, hd).transpose(1, 0, 2, 3, 4, 5)

    def block_fn(args):
        qn, n = args
        qpos = n * BLOCK + jnp.arange(BLOCK)
        s = jnp.einsum('bqhgd,bchd->bhgqc', qn, kc).astype(jnp.float32) * scale
        valid = cmp_end[None, :] <= qpos[:, None]
        s = jnp.where(valid, s, NEG)
        m = jnp.max(s, axis=-1, keepdims=True)
        e = jnp.where(valid, jnp.exp(s - m), 0.0)
        p = e / jnp.maximum(jnp.sum(e, axis=-1, keepdims=True), 1e-30)
        o_cmp = jnp.einsum('bhgqc,bchd->bqhgd', p.astype(vc.dtype), vc)
        imp = jnp.pad(jnp.sum(p, axis=2), ((0, 0), (0, 0), (0, 0), (front, pad_end)))
        imp_sel = jnp.einsum('bhqjr,r->bhqj', imp[..., map_idx], ov_w)
        cur = qpos // SEL_BLOCK
        sel_valid = sel_start[None, :] <= qpos[:, None]
        forced = (jj[None, :] == 0) | (jj[None, :] == cur[:, None]) | (jj[None, :] == cur[:, None] - 1)
        score = jnp.where(forced, FORCE, jnp.where(sel_valid, imp_sel, NEG))
        _, idx = lax.top_k(score, topk)
        kg = ks[bi, hi, idx]
        vg = vs[bi, hi, idx]
        s2 = jnp.einsum('bqhgd,bhqktd->bhgqkt', qn, kg).astype(jnp.float32) * scale
        kpos = idx[..., None] * SEL_BLOCK + jnp.arange(SEL_BLOCK)
        dist = qpos[None, None, :, None, None] - kpos
        bias = jnp.moveaxis(bias_c[t5_bucket(dist), hi5], -1, 2).astype(jnp.float32)
        s2 = jnp.where((dist >= 0)[:, :, None], s2 + bias, NEG)
        p2 = jax.nn.softmax(s2.reshape(s2.shape[:4] + (topk * SEL_BLOCK,)), axis=-1).reshape(s2.shape)
        o_slc = jnp.einsum('bhgqkt,bhqktd->bqhgd', p2.astype(vg.dtype), vg)
        return o_cmp, o_slc

    o_cmp, o_slc = lax.map(block_fn, (qb, jnp.arange(nblk)))
    o_cmp = o_cmp.transpose(1, 0, 2, 3, 4, 5).reshape(bsz, S, H, hd)
    o_slc = o_slc.transpose(1, 0, 2, 3, 4, 5).reshape(bsz, S, H, hd)
    o_win, _ = banded_attention(q, k_win, v_win, C_WINDOW - 1, rel_bias=rel_bias)
    gt = jax.nn.sigmoid(gates.astype(jnp.float32)).astype(q.dtype)
    return gt[..., 0:1] * o_cmp + gt[..., 1:2] * o_slc + gt[..., 2:3] * o_win


def setup_inputs(seed: int = 0) -> dict:
    key = jax.random.key(seed)
    ks = jax.random.split(key, 16)
    f32 = jnp.float32
    nrm = lambda k, shape, s: jax.random.normal(k, shape, f32) * s
    return {
        "x": nrm(ks[0], (BATCH, SEQ, D_MODEL), 1.0),
        "c": nrm(ks[1], (BATCH, D_MODEL), 1.0),
        "positions": jnp.broadcast_to(jnp.arange(SEQ, dtype=jnp.int32), (BATCH, SEQ)),
        "w_in": nrm(ks[2], (DEPTH, D_MODEL, N_IN), D_MODEL ** -0.5),
        "w_out": nrm(ks[3], (DEPTH, MIX_WIDTH, D_MODEL), DN_BETA * MIX_WIDTH ** -0.5),
        "rel_bias": nrm(ks[4], (REL_BUCKETS, A_HEADS + C_HEADS), 0.2),
        "sinks": nrm(ks[5], (DEPTH, B_HEADS), 1.0),
        "cmp_pos": nrm(ks[6], (DEPTH, 2, CMP_BLOCK, HEAD_DIM), 0.1),
        "cmp_w1": nrm(ks[7], (DEPTH, 2, CMP_BLOCK * HEAD_DIM, CMP_HIDDEN), (CMP_BLOCK * HEAD_DIM) ** -0.5),
        "cmp_w2": nrm(ks[8], (DEPTH, 2, CMP_HIDDEN, HEAD_DIM), CMP_HIDDEN ** -0.5),
        "mlp_w1": nrm(ks[9], (DEPTH, D_MODEL, D_FF), D_MODEL ** -0.5),
        "mlp_w2": nrm(ks[10], (DEPTH, D_FF, D_MODEL), DN_BETA * D_FF ** -0.5),
        "ada_w": nrm(ks[11], (DEPTH, D_MODEL, 6 * D_MODEL), 0.1 * D_MODEL ** -0.5),
        "ada_b": nrm(ks[12], (DEPTH, 6 * D_MODEL), 0.01),
        "ln_g": 1.0 + nrm(ks[13], (DEPTH, 2, D_MODEL), 0.02),
        "ln_b": nrm(ks[14], (DEPTH, 2, D_MODEL), 0.02),
    }


def reference(x, c, positions, w_in, w_out, rel_bias, sinks, cmp_pos, cmp_w1, cmp_w2,
              mlp_w1, mlp_w2, ada_w, ada_b, ln_g, ln_b):
    bsz, seq = x.shape[0], x.shape[1]
    heads = lambda t: t.reshape(bsz, seq, -1, HEAD_DIM)
    bias_a = rel_bias[:, :A_HEADS]
    bias_c = rel_bias[:, A_HEADS:]
    for l in range(DEPTH):
        mod = jax.nn.silu(c) @ ada_w[l] + ada_b[l]
        sh_a, sc_a, g_a, sh_m, sc_m, g_m = [m_[:, None, :] for m_ in jnp.split(mod, 6, axis=-1)]
        h = x * (1 + sc_a) + sh_a
        parts = jnp.split(h @ w_in[l], SPLIT_POINTS, axis=-1)
        aq, ak, av, bq, bk, bv, cq, ckc, cvc, cks, cvs, ckw, cvw = [heads(p) for p in parts[:-1]]
        cg = parts[-1].reshape(bsz, seq, C_HEADS, 3)
        oa = dilated_attention(aq, ak, av, bias_a)
        ob, _ = banded_attention(rope(bq, positions), rope(bk, positions), bv, B_WINDOW - 1, sinks=sinks[l])
        oc = nsa_attention(cq, ckc, cvc, cks, cvs, ckw, cvw, cg, cmp_pos[l], cmp_w1[l], cmp_w2[l], bias_c)
        mixed = jnp.concatenate([oa.reshape(bsz, seq, -1), ob.reshape(bsz, seq, -1),
                                 oc.reshape(bsz, seq, -1)], axis=-1)
        x = layer_norm(DN_ALPHA * x + (1 + g_a) * (mixed @ w_out[l]), ln_g[l, 0], ln_b[l, 0])
        h = x * (1 + sc_m) + sh_m
        f = jnp.square(jax.nn.relu(h @ mlp_w1[l])) @ mlp_w2[l]
        x = layer_norm(DN_ALPHA * x + (1 + g_m) * f, ln_g[l, 1], ln_b[l, 1])
    return x
```

```python
import functools
import math

import numpy as np
import jax
import jax.numpy as jnp
from jax import lax
from jax.experimental import pallas as pl
from jax.experimental.pallas import tpu as pltpu

F32 = jnp.float32
BF16 = jnp.bfloat16

D_MODEL = 1024
DEPTH = 4
HEAD_DIM = 64
LANES = 128
BLOCK = 128
A_CONFIGS = ((128, 1), (512, 4), (2048, 16))
B_WINDOW = 128
ROPE_THETA = 150000.0
CMP_BLOCK = 32
CMP_STRIDE = 16
CMP_HIDDEN = 256
SEL_BLOCK = 64
SEL_TOPK = 16
C_WINDOW = 512
REL_BUCKETS = 32
REL_MAX_DIST = 2048
D_FF = 4 * D_MODEL
DN_ALPHA = (2 * DEPTH) ** 0.25
LN_EPS = 1e-5
NEG = -1e30
FORCE = 1e4
SCALE = HEAD_DIM ** -0.5

COL_AQ, COL_AK, COL_AV = 0, 256, 512
COL_BQ, COL_BK, COL_BV = 768, 1024, 1152
COL_CQ = 1280
COL_CKC = 1792
COL_CKS, COL_CVS = 2048, 2176
COL_CKW, COL_CVW = 2304, 2432
COL_CG = 2560
N_QKV = 2560
N_IN = 2584
N_IN_PAD = 2688

SEL_TQ = 128
SEL_TK = 512
SEL_SLAB_LO = -3
SEL_SLAB_HI = 15
SEL_NSLAB = SEL_SLAB_HI - SEL_SLAB_LO + 2
VMEM_LIMIT = 56 * 1024 * 1024


def _cparams(sem):
    return pltpu.CompilerParams(dimension_semantics=sem, vmem_limit_bytes=VMEM_LIMIT)


def _dot_nt(a, b):
    return lax.dot_general(a, b, (((1,), (1,)), ((), ())), preferred_element_type=F32)


def _dot(a, b):
    return jnp.dot(a, b, preferred_element_type=F32)


def _mod_kernel(c_ref, w_ref, b_ref, o_ref):
    c = c_ref[...]
    act = c * jax.nn.sigmoid(c)
    lhs = jnp.broadcast_to(act, (8, D_MODEL)).astype(BF16)
    y = _dot(lhs, w_ref[...].astype(BF16))
    o_ref[...] = y[0:1, :] + b_ref[...]


def _modulation(c, ada_w, ada_b):
    tn = 1536
    return pl.pallas_call(
        _mod_kernel,
        out_shape=jax.ShapeDtypeStruct((DEPTH, 1, 6 * D_MODEL), F32),
        grid=(DEPTH, 6 * D_MODEL // tn),
        in_specs=[
            pl.BlockSpec((1, D_MODEL), lambda l, j: (0, 0)),
            pl.BlockSpec((None, D_MODEL, tn), lambda l, j: (l, 0, j)),
            pl.BlockSpec((None, 1, tn), lambda l, j: (l, 0, j)),
        ],
        out_specs=pl.BlockSpec((None, 1, tn), lambda l, j: (l, 0, j)),
        compiler_params=_cparams(("arbitrary", "arbitrary")),
        name="adaln_mod",
    )(c, ada_w, ada_b.reshape(DEPTH, 1, 6 * D_MODEL))


def _rope_table_kernel(pos_ref, freq_ref, cos_ref, sin_ref):
    ang = pos_ref[...].astype(F32) * freq_ref[...]
    cos_ref[...] = jnp.cos(ang)
    sin_ref[...] = jnp.sin(ang)


def _rope_tables(positions, seq):
    half = HEAD_DIM // 2
    freq = ROPE_THETA ** (-jnp.arange(half, dtype=F32) / half)
    freq = jnp.tile(freq, LANES // half).reshape(1, LANES)
    tm = 1024
    return pl.pallas_call(
        _rope_table_kernel,
        out_shape=(jax.ShapeDtypeStruct((seq, LANES), F32),) * 2,
        grid=(seq // tm,),
        in_specs=[pl.BlockSpec((tm, 1), lambda i: (i, 0)), pl.BlockSpec((1, LANES), lambda i: (0, 0))],
        out_specs=(pl.BlockSpec((tm, LANES), lambda i: (i, 0)),) * 2,
        compiler_params=_cparams(("arbitrary",)),
        name="rope_tables",
    )(positions.reshape(seq, 1), freq)


def _t5_bias(rb_ref, dist, col, max_dist):
    exact = REL_BUCKETS // 2
    d = jnp.maximum(dist, 0)
    df = jnp.maximum(d, 1).astype(F32)
    large = exact + (jnp.log(df / exact) / math.log(REL_MAX_DIST / exact)
                     * (REL_BUCKETS - exact)).astype(jnp.int32)
    large = jnp.minimum(large, REL_BUCKETS - 1)
    bucket = jnp.where(d < exact, d, large)
    out = jnp.zeros(dist.shape, F32)
    for b in range(REL_BUCKETS):
        out = jnp.where(bucket == b, rb_ref[b * 12 + col], out)
    return out, bucket


def _bias_a_kernel(rb_ref, o_ref):
    cfg = pl.program_id(0)
    head = pl.program_id(1)
    dil = jnp.where(cfg == 0, 1, jnp.where(cfg == 1, 4, 16))
    r = lax.broadcasted_iota(jnp.int32, (BLOCK, 2 * BLOCK), 0)
    c = lax.broadcasted_iota(jnp.int32, (BLOCK, 2 * BLOCK), 1)
    dist = r + BLOCK - c
    bias, _ = _t5_bias(rb_ref, dist * dil, head, None)
    o_ref[...] = jnp.where((dist >= 0) & (dist <= BLOCK), bias, NEG)


def _bias_w_kernel(rb_ref, o_ref):
    head = pl.program_id(0)
    nprev = C_WINDOW // BLOCK
    r = lax.broadcasted_iota(jnp.int32, (BLOCK, (nprev + 1) * BLOCK), 0)
    c = lax.broadcasted_iota(jnp.int32, (BLOCK, (nprev + 1) * BLOCK), 1)
    dist = r + nprev * BLOCK - c
    bias, _ = _t5_bias(rb_ref, dist, 4 + head, None)
    o_ref[...] = jnp.where((dist >= 0) & (dist <= C_WINDOW - 1), bias, NEG)


def _bias_s_kernel(rb_ref, o_ref):
    head = pl.program_id(0)
    slab = pl.program_id(1)
    r = lax.broadcasted_iota(jnp.int32, (BLOCK, BLOCK), 0)
    c = lax.broadcasted_iota(jnp.int32, (BLOCK, BLOCK), 1)
    dist = (slab + SEL_SLAB_LO) * BLOCK + r - c
    bias, _ = _t5_bias(rb_ref, dist, 4 + head, None)
    val = jnp.where(dist >= 0, bias, NEG)
    o_ref[...] = jnp.where(slab == SEL_NSLAB - 1, 0.0, val)


def _bias_tables(rel_bias):
    rb = rel_bias.reshape(-1)
    smem = pl.BlockSpec(memory_space=pltpu.SMEM)
    bias_a = pl.pallas_call(
        _bias_a_kernel,
        out_shape=jax.ShapeDtypeStruct((3, 4, BLOCK, 2 * BLOCK), F32),
        grid=(3, 4),
        in_specs=[smem],
        out_specs=pl.BlockSpec((None, None, BLOCK, 2 * BLOCK), lambda a, h: (a, h, 0, 0)),
        compiler_params=_cparams(("arbitrary", "arbitrary")),
        name="bias_table_dilated",
    )(rb)
    wctx = C_WINDOW + BLOCK
    bias_w = pl.pallas_call(
        _bias_w_kernel,
        out_shape=jax.ShapeDtypeStruct((8, BLOCK, wctx), F32),
        grid=(8,),
        in_specs=[smem],
        out_specs=pl.BlockSpec((None, BLOCK, wctx), lambda h: (h, 0, 0)),
        compiler_params=_cparams(("arbitrary",)),
        name="bias_table_window",
    )(rb)
    bias_s = pl.pallas_call(
        _bias_s_kernel,
        out_shape=jax.ShapeDtypeStruct((8, SEL_NSLAB, BLOCK, BLOCK), F32),
        grid=(8, SEL_NSLAB),
        in_specs=[smem],
        out_specs=pl.BlockSpec((None, None, BLOCK, BLOCK), lambda h, u: (h, u, 0, 0)),
        compiler_params=_cparams(("arbitrary", "arbitrary")),
        name="bias_table_selected",
    )(rb)
    return bias_a, bias_w, bias_s


def _rope_apply(t, cos, sin, first):
    up = pltpu.roll(t, LANES - HEAD_DIM // 2, axis=1)
    dn = pltpu.roll(t, HEAD_DIM // 2, axis=1)
    return t * cos + jnp.where(first, -up, dn) * sin


def _inproj_kernel(x_ref, sc_ref, sh_ref, w_ref, cos_ref, sin_ref, qkv_ref, ck_ref, g_ref):
    h = (x_ref[...] * (1 + sc_ref[...]) + sh_ref[...]).astype(BF16)
    y = _dot(h, w_ref[...])
    cos = cos_ref[...]
    sin = sin_ref[...]
    lane = lax.broadcasted_iota(jnp.int32, cos.shape, 1)
    first = (lane & (HEAD_DIM - 1)) < HEAD_DIM // 2

    def put(c0, width, val):
        qkv_ref[:, c0:c0 + width] = val.astype(BF16)

    put(COL_AQ, 256, y[:, COL_AQ:COL_AQ + 256] * SCALE)
    put(COL_AK, 512, y[:, COL_AK:COL_AK + 512])
    for j in range(2):
        c0 = COL_BQ + j * LANES
        put(c0, LANES, _rope_apply(y[:, c0:c0 + LANES], cos, sin, first) * SCALE)
    put(COL_BK, LANES, _rope_apply(y[:, COL_BK:COL_BK + LANES], cos, sin, first))
    put(COL_BV, LANES, y[:, COL_BV:COL_BV + LANES])
    put(COL_CQ, 512, y[:, COL_CQ:COL_CQ + 512] * SCALE)
    put(COL_CKC, 256, y[:, COL_CKC:COL_CKC + 256])
    put(COL_CKS, 512, y[:, COL_CKS:COL_CKS + 512])
    for g in range(4):
        c0 = COL_CKC + g * HEAD_DIM
        ck_ref[g] = y[:, c0:c0 + HEAD_DIM]
    g_ref[...] = y[:, COL_CG:COL_CG + LANES]


def _inproj(x, mod_l, w_in_l, cos_t, sin_t, seq):
    tm = 512
    vec = lambda k: pl.BlockSpec((1, D_MODEL), lambda i, k=k: (0, k))
    return pl.pallas_call(
        _inproj_kernel,
        out_shape=(jax.ShapeDtypeStruct((seq, N_QKV), BF16),
                   jax.ShapeDtypeStruct((4, seq, HEAD_DIM), F32),
                   jax.ShapeDtypeStruct((seq, LANES), F32)),
        grid=(seq // tm,),
        in_specs=[
            pl.BlockSpec((tm, D_MODEL), lambda i: (i, 0)),
            vec(1), vec(0),
            pl.BlockSpec((D_MODEL, N_IN_PAD), lambda i: (0, 0)),
            pl.BlockSpec((tm, LANES), lambda i: (i, 0)),
            pl.BlockSpec((tm, LANES), lambda i: (i, 0)),
        ],
        out_specs=(pl.BlockSpec((tm, N_QKV), lambda i: (i, 0)),
                   pl.BlockSpec((4, tm, HEAD_DIM), lambda i: (0, i, 0)),
                   pl.BlockSpec((tm, LANES), lambda i: (i, 0))),
        compiler_params=_cparams(("arbitrary",)),
        name="in_proj",
    )(x, mod_l, mod_l, w_in_l, cos_t, sin_t)


def _lane_half():
    return lax.broadcasted_iota(jnp.int32, (BLOCK, LANES), 1) >> 6


def _align_q(q, half, x, ysel):
    qx = jnp.where(half == x, q, 0.0)
    if ysel is None:
        return qx.astype(BF16)
    qd = qx + pltpu.roll(qx, HEAD_DIM, axis=1)
    return jnp.where(ysel, qd, 0.0).astype(BF16)


def _spread_o(o, ysel):
    if ysel is None:
        return o
    ob = jnp.where(ysel, o, 0.0)
    return ob + pltpu.roll(ob, HEAD_DIM, axis=1)


def _banded_kernel(*refs, nprev, tq, max_dist, use_bias, use_sinks, want_lse, group_div, row_axis):
    refs = list(refs)
    sink_ref = refs.pop(0) if use_sinks else None
    q_ref, kp_ref, kc_ref, vp_ref, vc_ref = refs[:5]
    refs = refs[5:]
    bias_ref = refs.pop(0) if use_bias else None
    o_ref = refs.pop(0)
    lse_ref = refs.pop(0) if want_lse else None
    kctx, vctx = refs

    tp = nprev * BLOCK
    ctx = (nprev + 1) * BLOCK
    i = pl.program_id(row_axis)
    p = pl.program_id(row_axis - 1)
    kctx[0:tp, :] = kp_ref[...]
    kctx[tp:tp + tq, :] = kc_ref[...]
    vctx[0:tp, :] = vp_ref[...]
    vctx[tp:tp + tq, :] = vc_ref[...]

    half = _lane_half()
    upper = half == 1
    ysel = None if group_div is None else half == p // group_div
    rr = lax.broadcasted_iota(jnp.int32, (BLOCK, ctx), 0)
    cc = lax.broadcasted_iota(jnp.int32, (BLOCK, ctx), 1)
    if not use_bias:
        dist = rr + tp - cc
        band = (dist >= 0) & (dist <= max_dist)

    for sub in range(tq // BLOCK):
        rows = slice(sub * BLOCK, (sub + 1) * BLOCK)
        q = q_ref[rows, :].astype(F32)
        k = kctx[sub * BLOCK:sub * BLOCK + ctx, :]
        v = vctx[sub * BLOCK:sub * BLOCK + ctx, :]
        low = tp - (i * tq + sub * BLOCK)
        inseq = cc >= low
        outs, lses = [], []
        for x in range(2):
            s = _dot_nt(_align_q(q, half, x, ysel), k)
            if use_bias:
                s = s + bias_ref[x]
            else:
                s = jnp.where(band, s, NEG)
            s = jnp.where(inseq, s, NEG)
            m = jnp.max(s, axis=1, keepdims=True)
            if use_sinks:
                sk = sink_ref[2 * p + x]
                m = jnp.maximum(m, sk)
            e = jnp.exp(s - m)
            den = jnp.sum(e, axis=1, keepdims=True)
            if use_sinks:
                den = den + jnp.exp(sk - m)
            o = _dot(e.astype(BF16), v) / den
            outs.append(_spread_o(o, ysel))
            lses.append(m + jnp.log(den))
        o_ref[rows, :] = jnp.where(upper, outs[1], outs[0])
        if want_lse:
            lse_ref[rows, :] = jnp.where(upper, jnp.broadcast_to(lses[1], (BLOCK, LANES)),
                                         jnp.broadcast_to(lses[0], (BLOCK, LANES)))


def _banded_call(qkv_view, nrows, nsub, row_width_blocks, q_cb, k_cb, v_cb, n_qblocks, *, nprev, tq,
                 max_dist, bias=None, sinks=None, want_lse=False, group_div=None, kv_per_qblock, name):
    tp = nprev * BLOCK
    ratio = tq // tp
    ctx = (nprev + 1) * BLOCK

    def kvb(p):
        return p if kv_per_qblock else 0

    in_specs = []
    args = []
    if sinks is not None:
        in_specs.append(pl.BlockSpec(memory_space=pltpu.SMEM))
        args.append(sinks)
    in_specs += [
        pl.BlockSpec((tq, LANES), lambda r, p, i: (i, r * row_width_blocks + q_cb + p)),
        pl.BlockSpec((tp, LANES), lambda r, p, i: (jnp.maximum(i * ratio - 1, 0), r * row_width_blocks + k_cb + kvb(p))),
        pl.BlockSpec((tq, LANES), lambda r, p, i: (i, r * row_width_blocks + k_cb + kvb(p))),
        pl.BlockSpec((tp, LANES), lambda r, p, i: (jnp.maximum(i * ratio - 1, 0), r * row_width_blocks + v_cb + kvb(p))),
        pl.BlockSpec((tq, LANES), lambda r, p, i: (i, r * row_width_blocks + v_cb + kvb(p))),
    ]
    args += [qkv_view] * 5
    if bias is not None:
        in_specs.append(pl.BlockSpec((2, BLOCK, ctx), lambda r, p, i: (p, 0, 0)))
        args.append(bias)
    out_w = nsub * n_qblocks * LANES
    o_spec = pl.BlockSpec((tq, LANES), lambda r, p, i: (i, r * n_qblocks + p))
    out_shape = [jax.ShapeDtypeStruct((nrows, out_w), F32)]
    out_specs = [o_spec]
    if want_lse:
        out_shape.append(jax.ShapeDtypeStruct((nrows, out_w), F32))
        out_specs.append(o_spec)
    kern = functools.partial(_banded_kernel, nprev=nprev, tq=tq, max_dist=max_dist, use_bias=bias is not None,
                             use_sinks=sinks is not None, want_lse=want_lse, group_div=group_div, row_axis=2)
    return pl.pallas_call(
        kern,
        out_shape=tuple(out_shape),
        grid=(nsub, n_qblocks, nrows // tq),
        in_specs=in_specs,
        out_specs=tuple(out_specs),
        scratch_shapes=[pltpu.VMEM((tp + tq, LANES), BF16), pltpu.VMEM((tp + tq, LANES), BF16)],
        compiler_params=_cparams(("arbitrary", "arbitrary", "arbitrary")),
        name=name,
    )(*args)


def _compress_kernel(x_ref, pos_ref, w1_ref, w2_ref, o_ref):
    x = x_ref[...]
    half = CMP_STRIDE * HEAD_DIM
    xa = (x + pos_ref[0:1, :]).astype(BF16)
    xb = (x + pos_ref[1:2, :]).astype(BF16)
    first = _dot(xa, w1_ref[0:half, :])
    second = _dot(xb, w1_ref[half:2 * half, :])
    n = x.shape[0]
    hid = first + pltpu.roll(second, n - 1, axis=0)
    act = jax.nn.gelu(hid)
    o_ref[...] = _dot(act.astype(BF16), w2_ref[...]).astype(BF16)


def _compress(ck, cmp_pos_l, w1_l, w2_l, seq):
    n_chunk = seq // CMP_STRIDE
    feat = CMP_STRIDE * HEAD_DIM
    x = ck.reshape(4, n_chunk, feat)
    pos = cmp_pos_l.reshape(2, 2, feat)
    out = pl.pallas_call(
        _compress_kernel,
        out_shape=jax.ShapeDtypeStruct((4, n_chunk, HEAD_DIM), BF16),
        grid=(4,),
        in_specs=[
            pl.BlockSpec((None, n_chunk, feat), lambda g: (g, 0, 0)),
            pl.BlockSpec((None, 2, feat), lambda g: (g // 2, 0, 0)),
            pl.BlockSpec((None, 2 * feat, CMP_HIDDEN), lambda g: (g // 2, 0, 0)),
            pl.BlockSpec((None, CMP_HIDDEN, HEAD_DIM), lambda g: (g // 2, 0, 0)),
        ],
        out_specs=pl.BlockSpec((None, n_chunk, HEAD_DIM), lambda g: (g, 0, 0)),
        compiler_params=_cparams(("arbitrary",)),
        name="compress_tokens",
    )(x, pos, w1_l, w2_l)
    kc = jnp.concatenate([out[0], out[1]], axis=1)
    vc = jnp.concatenate([out[2], out[3]], axis=1)
    return kc, vc


def _cmp_attn_kernel(q_ref, kc_ref, vc_ref, mt_ref, o_ref, mq_ref, *, n_cmp, n_blk):
    kvh = pl.program_id(0)
    qi = pl.program_id(1)
    half = _lane_half()
    upper = half == 1
    ysel = half == kvh
    qpos = qi * BLOCK + lax.broadcasted_iota(jnp.int32, (BLOCK, n_cmp), 0)
    cend = lax.broadcasted_iota(jnp.int32, (BLOCK, n_cmp), 1) * CMP_STRIDE + (CMP_BLOCK - 1)
    valid = cend <= qpos
    kc = kc_ref[...]
    vc = vc_ref[...]
    imp = jnp.zeros((BLOCK, n_cmp), F32)
    outs = []
    for h in range(4):
        j, x = divmod(h, 2)
        q = q_ref[:, j * LANES:(j + 1) * LANES].astype(F32)
        s = _dot_nt(_align_q(q, half, x, ysel), kc)
        s = jnp.where(valid, s, NEG)
        m = jnp.max(s, axis=1, keepdims=True)
        e = jnp.where(valid, jnp.exp(s - m), 0.0)
        prob = e / jnp.maximum(jnp.sum(e, axis=1, keepdims=True), 1e-30)
        imp = imp + prob
        outs.append(_spread_o(_dot(prob.astype(BF16), vc), ysel))
    o_ref[:, 0:LANES] = jnp.where(upper, outs[1], outs[0])
    o_ref[:, LANES:2 * LANES] = jnp.where(upper, outs[3], outs[2])

    hi = imp.astype(BF16)
    r1 = imp - hi.astype(F32)
    mid = r1.astype(BF16)
    lo = (r1 - mid.astype(F32)).astype(BF16)
    mt = mt_ref[...]
    imp_sel = _dot_nt(mt, hi) + _dot_nt(mt, mid) + _dot_nt(mt, lo)
    blk = lax.broadcasted_iota(jnp.int32, (n_blk, BLOCK), 0)
    cur = (qi * BLOCK + lax.broadcasted_iota(jnp.int32, (n_blk, BLOCK), 1)) >> 6
    forced = (blk == 0) | (blk == cur) | (blk == cur - 1)
    causal = blk <= cur
    score = jnp.where(forced, FORCE, jnp.where(causal, imp_sel, NEG))
    blk_f = blk.astype(F32)
    chosen = jnp.zeros((n_blk, BLOCK), F32)
    for _ in range(SEL_TOPK):
        mx = jnp.max(score, axis=0, keepdims=True)
        first = jnp.min(jnp.where(score == mx, blk_f, 1e9), axis=0, keepdims=True)
        pick = blk_f == first
        chosen = jnp.where(pick, 1.0, chosen)
        score = jnp.where(pick, -jnp.inf, score)
    chosen = jnp.where(causal, chosen, 0.0).astype(BF16)
    eye = (lax.broadcasted_iota(jnp.int32, (BLOCK, BLOCK), 0)
           == lax.broadcasted_iota(jnp.int32, (BLOCK, BLOCK), 1)).astype(BF16)
    sel = _dot_nt(eye, chosen)
    mq_ref[...] = ((sel - 1.0) * (-NEG)).astype(BF16)


def _importance_matrix(n_blk, n_cmp):
    mt = np.zeros((n_blk, n_cmp), np.float32)
    ratio = SEL_BLOCK // CMP_STRIDE
    for o in range(-(CMP_BLOCK // CMP_STRIDE - 1), ratio):
        w = max(0, min(SEL_BLOCK, o * CMP_STRIDE + CMP_BLOCK) - max(0, o * CMP_STRIDE)) / CMP_BLOCK
        for j in range(n_blk):
            c = j * ratio + o
            if 0 <= c < n_cmp and j * SEL_BLOCK < n_cmp * CMP_STRIDE:
                mt[j, c] = w
    return jnp.asarray(mt, dtype=BF16)


def _cmp_attn(qkv, kc, vc, seq):
    n_cmp = seq // CMP_STRIDE
    n_blk = max(LANES, seq // SEL_BLOCK)
    mt = _importance_matrix(n_blk, n_cmp)
    kern = functools.partial(_cmp_attn_kernel, n_cmp=n_cmp, n_blk=n_blk)
    return pl.pallas_call(
        kern,
        out_shape=(jax.ShapeDtypeStruct((seq, 512), F32), jax.ShapeDtypeStruct((2, seq, n_blk), BF16)),
        grid=(2, seq // BLOCK),
        in_specs=[
            pl.BlockSpec((BLOCK, 256), lambda h, i: (i, COL_CQ // 256 + h)),
            pl.BlockSpec((n_cmp, LANES), lambda h, i: (0, 0)),
            pl.BlockSpec((n_cmp, LANES), lambda h, i: (0, 0)),
            pl.BlockSpec((n_blk, n_cmp), lambda h, i: (0, 0)),
        ],
        out_specs=(pl.BlockSpec((BLOCK, 256), lambda h, i: (i, h)),
                   pl.BlockSpec((None, BLOCK, n_blk), lambda h, i: (h, i, 0))),
        compiler_params=_cparams(("arbitrary", "arbitrary")),
        name="cmp_attn_topk",
    )(qkv, kc, vc, mt)


def _sel_kernel(qi_ref, kt_ref, q_ref, mq_ref, k_ref, v_ref, e_ref, tab_ref, b31_ref, o_ref,
                qaug, m_s, l_s, acc_s):
    kvh = pl.program_id(0)
    step = pl.program_id(1)
    qi = qi_ref[step]
    kt = kt_ref[step]
    half = _lane_half()
    upper = half == 1
    ysel = half == kvh
    per_q = SEL_TK // SEL_TQ
    win_tiles = LANES * SEL_BLOCK // SEL_TK

    @pl.when(kt == 0)
    def _():
        m_s[...] = jnp.full(m_s.shape, -3e38, F32)
        l_s[...] = jnp.zeros(l_s.shape, F32)
        acc_s[...] = jnp.zeros(acc_s.shape, F32)

    @pl.when(kt % win_tiles == 0)
    def _():
        mq = mq_ref[...]
        for h in range(4):
            j, x = divmod(h, 2)
            q = q_ref[:, j * LANES:(j + 1) * LANES].astype(F32)
            qaug[h * BLOCK:(h + 1) * BLOCK, 0:LANES] = _align_q(q, half, x, ysel)
            qaug[h * BLOCK:(h + 1) * BLOCK, LANES:2 * LANES] = mq

    kt_near0 = jnp.maximum(0, (qi - SEL_SLAB_HI + per_q - 1) // per_q)

    @pl.when(kt == kt_near0)
    def _():
        for h in range(4):
            rows = slice(h * BLOCK, (h + 1) * BLOCK)
            m_s[rows, :] = m_s[rows, :] + b31_ref[kvh * 4 + h]

    kaug = jnp.concatenate([k_ref[...], e_ref[...]], axis=1)
    s = _dot_nt(qaug[...], kaug)
    is_far = kt < kt_near0
    parts = []
    for h in range(4):
        cols = []
        for t in range(SEL_TK // BLOCK):
            u = qi - per_q * kt - t
            uidx = jnp.where(is_far, SEL_NSLAB - 1, u - SEL_SLAB_LO)
            cols.append(s[h * BLOCK:(h + 1) * BLOCK, t * BLOCK:(t + 1) * BLOCK] + tab_ref[h, uidx])
        parts.append(jnp.concatenate(cols, axis=1))
    s = jnp.concatenate(parts, axis=0)

    m_prev = m_s[...]
    m_new = jnp.maximum(m_prev, jnp.max(s, axis=1, keepdims=True))
    alpha = jnp.exp(m_prev - m_new)
    pr = jnp.exp(s - m_new)
    l_s[...] = alpha * l_s[...] + jnp.sum(pr, axis=1, keepdims=True)
    acc_s[...] = alpha * acc_s[...] + _dot(pr.astype(BF16), v_ref[...])
    m_s[...] = m_new

    @pl.when(kt == qi // per_q)
    def _():
        o = acc_s[...] / l_s[...]
        outs = [_spread_o(o[h * BLOCK:(h + 1) * BLOCK, :], ysel) for h in range(4)]
        o_ref[:, 0:LANES] = jnp.where(upper, outs[1], outs[0])
        o_ref[:, LANES:2 * LANES] = jnp.where(upper, outs[3], outs[2])


def _sel_attn(qkv, mq, bias_s, rel_bias, seq):
    n_q = seq // SEL_TQ
    per_q = SEL_TK // SEL_TQ
    qi_l, kt_l = [], []
    for qi in range(n_q):
        for kt in range(qi // per_q + 1):
            qi_l.append(qi)
            kt_l.append(kt)
    qi_arr = jnp.asarray(np.array(qi_l, np.int32))
    kt_arr = jnp.asarray(np.array(kt_l, np.int32))
    win_tiles = LANES * SEL_BLOCK // SEL_TK
    n_e = min(win_tiles, seq // SEL_TK)
    e_np = np.zeros((n_e, SEL_TK, LANES), np.float32)
    for t in range(n_e):
        for k in range(SEL_TK):
            e_np[t, k, (t * (SEL_TK // SEL_BLOCK) + k // SEL_BLOCK) % LANES] = 1.0
    e_all = jnp.asarray(e_np, dtype=BF16)
    b31 = rel_bias[REL_BUCKETS - 1, 4:12]
    grid_spec = pltpu.PrefetchScalarGridSpec(
        num_scalar_prefetch=2,
        grid=(2, len(qi_l)),
        in_specs=[
            pl.BlockSpec((SEL_TQ, 256), lambda h, s, qi, kt: (qi[s], COL_CQ // 256 + h)),
            pl.BlockSpec((None, SEL_TQ, LANES), lambda h, s, qi, kt: (h, qi[s], kt[s] // win_tiles)),
            pl.BlockSpec((SEL_TK, LANES), lambda h, s, qi, kt: (kt[s], COL_CKS // LANES)),
            pl.BlockSpec((SEL_TK, LANES), lambda h, s, qi, kt: (kt[s], COL_CVS // LANES)),
            pl.BlockSpec((None, SEL_TK, LANES), lambda h, s, qi, kt: (kt[s] % n_e, 0, 0)),
            pl.BlockSpec((4, SEL_NSLAB, BLOCK, BLOCK), lambda h, s, qi, kt: (h, 0, 0, 0)),
            pl.BlockSpec(memory_space=pltpu.SMEM),
        ],
        out_specs=pl.BlockSpec((SEL_TQ, 256), lambda h, s, qi, kt: (qi[s], h)),
        scratch_shapes=[pltpu.VMEM((4 * SEL_TQ, 2 * LANES), BF16),
                        pltpu.VMEM((4 * SEL_TQ, 1), F32),
                        pltpu.VMEM((4 * SEL_TQ, 1), F32),
                        pltpu.VMEM((4 * SEL_TQ, LANES), F32)],
    )
    return pl.pallas_call(
        _sel_kernel,
        out_shape=jax.ShapeDtypeStruct((seq, 512), F32),
        grid_spec=grid_spec,
        compiler_params=_cparams(("arbitrary", "arbitrary")),
        name="selected_attn",
    )(qi_arr, kt_arr, qkv, mq, qkv, qkv, e_all, bias_s, b31)


def _layer_norm(z, g, b):
    mu = jnp.mean(z, axis=-1, keepdims=True)
    zc = z - mu
    var = jnp.mean(jnp.square(zc), axis=-1, keepdims=True)
    return zc * lax.rsqrt(var + LN_EPS) * g + b


def _outproj_kernel(x_ref, oa0, oa1, oa2, la0, la1, la2, ob_ref, ocmp_ref, oslc_ref, owin_ref, g_ref,
                    gx_ref, w_ref, ga_ref, lng_ref, lnb_ref, o_ref):
    l0, l1, l2 = la0[...], la1[...], la2[...]
    mx = jnp.maximum(jnp.maximum(l0, l1), l2)
    w0, w1, w2 = jnp.exp(l0 - mx), jnp.exp(l1 - mx), jnp.exp(l2 - mx)
    oa = (w0 * oa0[...] + w1 * oa1[...] + w2 * oa2[...]) / (w0 + w1 + w2)
    gt = jax.nn.sigmoid(g_ref[...])
    hi = gt.astype(BF16)
    lo = (gt - hi.astype(F32)).astype(BF16)
    gx = gx_ref[...]
    gates = _dot(hi, gx) + _dot(lo, gx)
    oc = gates[:, 0:512] * ocmp_ref[...] + gates[:, 512:1024] * oslc_ref[...] + gates[:, 1024:1536] * owin_ref[...]
    mixed = jnp.concatenate([oa, ob_ref[...], oc], axis=1).astype(BF16)
    y = _dot(mixed, w_ref[...])
    z = DN_ALPHA * x_ref[...] + (1 + ga_ref[...]) * y
    o_ref[...] = _layer_norm(z, lng_ref[...], lnb_ref[...])


def _gate_expand():
    gx = np.zeros((LANES, 3 * 512), np.float32)
    for h in range(8):
        for b in range(3):
            gx[h * 3 + b, b * 512 + h * HEAD_DIM:b * 512 + (h + 1) * HEAD_DIM] = 1.0
    return jnp.asarray(gx, dtype=BF16)


def _outproj(x, oa, la, ob, ocmp, oslc, owin, gates, w_out_l, mod_l, lng, lnb, seq):
    tm = 256
    row = lambda w: pl.BlockSpec((tm, w), lambda i: (i, 0))
    vec = lambda k: pl.BlockSpec((1, D_MODEL), lambda i, k=k: (0, k))
    return pl.pallas_call(
        _outproj_kernel,
        out_shape=jax.ShapeDtypeStruct((seq, D_MODEL), F32),
        grid=(seq // tm,),
        in_specs=[row(D_MODEL)] + [row(256)] * 7 + [row(512)] * 3 + [row(LANES)] + [
            pl.BlockSpec((LANES, 3 * 512), lambda i: (0, 0)),
            pl.BlockSpec((D_MODEL, D_MODEL), lambda i: (0, 0)),
            vec(2), vec(0), vec(0)],
        out_specs=row(D_MODEL),
        compiler_params=_cparams(("arbitrary",)),
        name="out_proj_ln",
    )(x, oa[0], oa[1], oa[2], la[0], la[1], la[2], ob, ocmp, oslc, owin, gates, _gate_expand(), w_out_l,
      mod_l, lng, lnb)


def _mlp_kernel(x_ref, sc_ref, sh_ref, g_ref, w1_ref, w2_ref, lng_ref, lnb_ref, o_ref, h_s, acc_s):
    j = pl.program_id(1)

    @pl.when(j == 0)
    def _():
        h_s[...] = (x_ref[...] * (1 + sc_ref[...]) + sh_ref[...]).astype(BF16)
        acc_s[...] = jnp.zeros(acc_s.shape, F32)

    f = jnp.maximum(_dot(h_s[...], w1_ref[...]), 0.0)
    acc_s[...] += _dot(jnp.square(f).astype(BF16), w2_ref[...])

    @pl.when(j == pl.num_programs(1) - 1)
    def _():
        z = DN_ALPHA * x_ref[...] + (1 + g_ref[...]) * acc_s[...]
        o_ref[...] = _layer_norm(z, lng_ref[...], lnb_ref[...])


def _mlp(x, mod_l, w1_l, w2_l, lng, lnb, seq):
    tm = 1024
    tf = 1024
    vec = lambda k: pl.BlockSpec((1, D_MODEL), lambda i, j, k=k: (0, k))
    return pl.pallas_call(
        _mlp_kernel,
        out_shape=jax.ShapeDtypeStruct((seq, D_MODEL), F32),
        grid=(seq // tm, D_FF // tf),
        in_specs=[
            pl.BlockSpec((tm, D_MODEL), lambda i, j: (i, 0)),
            vec(4), vec(3), vec(5),
            pl.BlockSpec((D_MODEL, tf), lambda i, j: (0, j)),
            pl.BlockSpec((tf, D_MODEL), lambda i, j: (j, 0)),
            vec(0), vec(0),
        ],
        out_specs=pl.BlockSpec((tm, D_MODEL), lambda i, j: (i, 0)),
        scratch_shapes=[pltpu.VMEM((tm, D_MODEL), BF16), pltpu.VMEM((tm, D_MODEL), F32)],
        compiler_params=_cparams(("arbitrary", "arbitrary")),
        name="mlp_ln",
    )(x, mod_l, mod_l, mod_l, w1_l, w2_l, lng, lnb)


def kernel(x, c, positions, w_in, w_out, rel_bias, sinks, cmp_pos, cmp_w1, cmp_w2, mlp_w1, mlp_w2,
           ada_w, ada_b, ln_g, ln_b):
    bsz, seq, d = x.shape
    assert bsz == 1 and d == D_MODEL and seq % (2 * A_CONFIGS[-1][0]) == 0
    x = x.reshape(seq, D_MODEL)

    mod = _modulation(c, ada_w, ada_b)
    cos_t, sin_t = _rope_tables(positions, seq)
    bias_a, bias_w, bias_s = _bias_tables(rel_bias)

    w_in_b = jnp.pad(w_in, ((0, 0), (0, 0), (0, N_IN_PAD - N_IN))).astype(BF16)
    w_out_b = w_out.astype(BF16)
    w1_b = mlp_w1.astype(BF16)
    w2_b = mlp_w2.astype(BF16)
    cw1_b = cmp_w1.astype(BF16)
    cw2_b = cmp_w2.astype(BF16)

    for l in range(DEPTH):
        mod_l = mod[l]
        qkv, ck, gates = _inproj(x, mod_l, w_in_b[l], cos_t, sin_t, seq)

        oa, la = [], []
        for ci, (window, dil) in enumerate(A_CONFIGS):
            nrows = seq // dil
            view = qkv.reshape(nrows, dil * N_QKV)
            o_c, l_c = _banded_call(
                view, nrows, dil, N_QKV // LANES, COL_AQ // LANES, COL_AK // LANES, COL_AV // LANES, 2,
                nprev=1, tq=256, max_dist=window // dil, bias=bias_a[ci], want_lse=True,
                group_div=None, kv_per_qblock=True, name=f"dilated_attn_{dil}")
            oa.append(o_c.reshape(seq, 256))
            la.append(l_c.reshape(seq, 256))

        (ob,) = _banded_call(
            qkv, seq, 1, N_QKV // LANES, COL_BQ // LANES, COL_BK // LANES, COL_BV // LANES, 2,
            nprev=1, tq=256, max_dist=B_WINDOW - 1, sinks=sinks[l], group_div=1,
            kv_per_qblock=False, name="swa_attn")

        kc, vc = _compress(ck, cmp_pos[l], cw1_b[l], cw2_b[l], seq)
        ocmp, mq = _cmp_attn(qkv, kc, vc, seq)
        oslc = _sel_attn(qkv, mq, bias_s, rel_bias, seq)
        (owin,) = _banded_call(
            qkv, seq, 1, N_QKV // LANES, COL_CQ // LANES, COL_CKW // LANES, COL_CVW // LANES, 4,
            nprev=C_WINDOW // BLOCK, tq=C_WINDOW, max_dist=C_WINDOW - 1, bias=bias_w, group_div=2,
            kv_per_qblock=False, name="window_attn")

        x = _outproj(x, oa, la, ob, ocmp, oslc, owin, gates, w_out_b[l], mod_l,
                     ln_g[l, 0:1], ln_b[l, 0:1], seq)
        x = _mlp(x, mod_l, w1_b[l], w2_b[l], ln_g[l, 1:2], ln_b[l, 1:2], seq)

    return x.reshape(bsz, seq, D_MODEL)
```

```python
import functools
import math

import numpy as np
import jax
import jax.numpy as jnp
from jax import lax
from jax.experimental import pallas as pl
from jax.experimental.pallas import tpu as pltpu

F32 = jnp.float32
BF16 = jnp.bfloat16

D_MODEL = 1024
DEPTH = 4
HEAD_DIM = 64
LANES = 128
BLOCK = 128
A_CONFIGS = ((128, 1), (512, 4), (2048, 16))
B_WINDOW = 128
ROPE_THETA = 150000.0
CMP_BLOCK = 32
CMP_STRIDE = 16
CMP_HIDDEN = 256
SEL_BLOCK = 64
SEL_TOPK = 16
C_WINDOW = 512
REL_BUCKETS = 32
REL_MAX_DIST = 2048
D_FF = 4 * D_MODEL
DN_ALPHA = (2 * DEPTH) ** 0.25
LN_EPS = 1e-5
NEG = -1e30
FORCE = 1e4
SCALE = HEAD_DIM ** -0.5

COL_AQ, COL_AK, COL_AV = 0, 256, 512
COL_BQ, COL_BK, COL_BV = 768, 1024, 1152
COL_CQ = 1280
COL_CKC = 1792
COL_CKS, COL_CVS = 2048, 2176
COL_CKW, COL_CVW = 2304, 2432
COL_CG = 2560
N_QKV = 2560
N_IN = 2584
N_IN_PAD = 2688

SEL_TQ = 128
SEL_SUB = 512
SEL_CHUNK = 128
SEL_NSUB = 4
SEL_TK = SEL_SUB * SEL_NSUB
SEL_SLAB_LO = -1
SEL_SLAB_HI = 15
SEL_NSLAB = SEL_SLAB_HI - SEL_SLAB_LO + 2
VMEM_LIMIT = 56 * 1024 * 1024


def _cparams(sem, flags=None):
    return pltpu.CompilerParams(dimension_semantics=sem, vmem_limit_bytes=VMEM_LIMIT, flags=flags)


def _dot_nt(a, b):
    return lax.dot_general(a, b, (((1,), (1,)), ((), ())), preferred_element_type=F32)


def _dot(a, b):
    return jnp.dot(a, b, preferred_element_type=F32)


def _mod_kernel(c_ref, w_ref, b_ref, o_ref):
    c = c_ref[...]
    act = c * jax.nn.sigmoid(c)
    lhs = jnp.broadcast_to(act, (8, D_MODEL)).astype(BF16)
    y = _dot(lhs, w_ref[...].astype(BF16))
    o_ref[...] = y[0:1, :] + b_ref[...]


def _modulation(c, ada_w, ada_b):
    tn = 1536
    return pl.pallas_call(
        _mod_kernel,
        out_shape=jax.ShapeDtypeStruct((DEPTH, 1, 6 * D_MODEL), F32),
        grid=(DEPTH, 6 * D_MODEL // tn),
        in_specs=[
            pl.BlockSpec((1, D_MODEL), lambda l, j: (0, 0)),
            pl.BlockSpec((None, D_MODEL, tn), lambda l, j: (l, 0, j)),
            pl.BlockSpec((None, 1, tn), lambda l, j: (l, 0, j)),
        ],
        out_specs=pl.BlockSpec((None, 1, tn), lambda l, j: (l, 0, j)),
        compiler_params=_cparams(("arbitrary", "arbitrary")),
        name="adaln_mod",
    )(c, ada_w, ada_b.reshape(DEPTH, 1, 6 * D_MODEL))


def _rope_table_kernel(pos_ref, freq_ref, cos_ref, sin_ref):
    ang = pos_ref[...].astype(F32) * freq_ref[...]
    cos_ref[...] = jnp.cos(ang)
    sin_ref[...] = jnp.sin(ang)


def _rope_tables(positions, seq):
    half = HEAD_DIM // 2
    freq = ROPE_THETA ** (-jnp.arange(half, dtype=F32) / half)
    freq = jnp.tile(freq, LANES // half).reshape(1, LANES)
    tm = 1024
    return pl.pallas_call(
        _rope_table_kernel,
        out_shape=(jax.ShapeDtypeStruct((seq, LANES), F32),) * 2,
        grid=(seq // tm,),
        in_specs=[pl.BlockSpec((tm, 1), lambda i: (i, 0)), pl.BlockSpec((1, LANES), lambda i: (0, 0))],
        out_specs=(pl.BlockSpec((tm, LANES), lambda i: (i, 0)),) * 2,
        compiler_params=_cparams(("arbitrary",)),
        name="rope_tables",
    )(positions.reshape(seq, 1), freq)


def _t5_bias(rb_ref, dist, col, max_dist):
    exact = REL_BUCKETS // 2
    d = jnp.maximum(dist, 0)
    df = jnp.maximum(d, 1).astype(F32)
    large = exact + (jnp.log(df / exact) / math.log(REL_MAX_DIST / exact)
                     * (REL_BUCKETS - exact)).astype(jnp.int32)
    large = jnp.minimum(large, REL_BUCKETS - 1)
    bucket = jnp.where(d < exact, d, large)
    out = jnp.zeros(dist.shape, F32)
    for b in range(REL_BUCKETS):
        out = jnp.where(bucket == b, rb_ref[b * 12 + col], out)
    return out, bucket


def _bias_a_kernel(rb_ref, o_ref):
    cfg = pl.program_id(0)
    head = pl.program_id(1)
    dil = jnp.where(cfg == 0, 1, jnp.where(cfg == 1, 4, 16))
    r = lax.broadcasted_iota(jnp.int32, (BLOCK, 2 * BLOCK), 0)
    c = lax.broadcasted_iota(jnp.int32, (BLOCK, 2 * BLOCK), 1)
    dist = r + BLOCK - c
    bias, _ = _t5_bias(rb_ref, dist * dil, head, None)
    o_ref[...] = jnp.where((dist >= 0) & (dist <= BLOCK), bias, NEG)


def _bias_w_kernel(rb_ref, o_ref):
    head = pl.program_id(0)
    nprev = C_WINDOW // BLOCK
    r = lax.broadcasted_iota(jnp.int32, (BLOCK, (nprev + 1) * BLOCK), 0)
    c = lax.broadcasted_iota(jnp.int32, (BLOCK, (nprev + 1) * BLOCK), 1)
    dist = r + nprev * BLOCK - c
    bias, _ = _t5_bias(rb_ref, dist, 4 + head, None)
    o_ref[...] = jnp.where((dist >= 0) & (dist <= C_WINDOW - 1), bias, NEG)


def _bias_s_kernel(rb_ref, o_ref):
    head = pl.program_id(0)
    slab = pl.program_id(1)
    r = lax.broadcasted_iota(jnp.int32, (BLOCK, BLOCK), 1)
    c = lax.broadcasted_iota(jnp.int32, (BLOCK, BLOCK), 0)
    dist = (slab + SEL_SLAB_LO) * BLOCK + r - c
    bias, _ = _t5_bias(rb_ref, dist, 4 + head, None)
    val = jnp.where(dist >= 0, bias, NEG)
    o_ref[...] = jnp.where(slab == SEL_NSLAB - 1, 0.0, val)


def _bias_tables(rel_bias):
    rb = rel_bias.reshape(-1)
    smem = pl.BlockSpec(memory_space=pltpu.SMEM)
    bias_a = pl.pallas_call(
        _bias_a_kernel,
        out_shape=jax.ShapeDtypeStruct((3, 4, BLOCK, 2 * BLOCK), F32),
        grid=(3, 4),
        in_specs=[smem],
        out_specs=pl.BlockSpec((None, None, BLOCK, 2 * BLOCK), lambda a, h: (a, h, 0, 0)),
        compiler_params=_cparams(("arbitrary", "arbitrary")),
        name="bias_table_dilated",
    )(rb)
    wctx = C_WINDOW + BLOCK
    bias_w = pl.pallas_call(
        _bias_w_kernel,
        out_shape=jax.ShapeDtypeStruct((8, BLOCK, wctx), F32),
        grid=(8,),
        in_specs=[smem],
        out_specs=pl.BlockSpec((None, BLOCK, wctx), lambda h: (h, 0, 0)),
        compiler_params=_cparams(("arbitrary",)),
        name="bias_table_window",
    )(rb)
    bias_s = pl.pallas_call(
        _bias_s_kernel,
        out_shape=jax.ShapeDtypeStruct((8, SEL_NSLAB, BLOCK, BLOCK), F32),
        grid=(8, SEL_NSLAB),
        in_specs=[smem],
        out_specs=pl.BlockSpec((None, None, BLOCK, BLOCK), lambda h, u: (h, u, 0, 0)),
        compiler_params=_cparams(("arbitrary", "arbitrary")),
        name="bias_table_selected",
    )(rb)
    return bias_a, bias_w, bias_s


def _rope_apply(t, cos, sin, first):
    up = pltpu.roll(t, LANES - HEAD_DIM // 2, axis=1)
    dn = pltpu.roll(t, HEAD_DIM // 2, axis=1)
    return t * cos + jnp.where(first, -up, dn) * sin


def _inproj_kernel(x_ref, sc_ref, sh_ref, w_ref, cos_ref, sin_ref, qkv_ref, ck_ref, g_ref, qt_ref, vt_ref):
    h = (x_ref[...] * (1 + sc_ref[...]) + sh_ref[...]).astype(BF16)
    y = _dot(h, w_ref[...])
    cos = cos_ref[...]
    sin = sin_ref[...]
    lane = lax.broadcasted_iota(jnp.int32, cos.shape, 1)
    first = (lane & (HEAD_DIM - 1)) < HEAD_DIM // 2

    def put(c0, width, val):
        qkv_ref[:, c0:c0 + width] = val.astype(BF16)

    put(COL_AQ, 256, y[:, COL_AQ:COL_AQ + 256] * SCALE)
    put(COL_AK, 512, y[:, COL_AK:COL_AK + 512])
    for j in range(2):
        c0 = COL_BQ + j * LANES
        put(c0, LANES, _rope_apply(y[:, c0:c0 + LANES], cos, sin, first) * SCALE)
    put(COL_BK, LANES, _rope_apply(y[:, COL_BK:COL_BK + LANES], cos, sin, first))
    put(COL_BV, LANES, y[:, COL_BV:COL_BV + LANES])
    put(COL_CQ, 512, y[:, COL_CQ:COL_CQ + 512] * SCALE)
    put(COL_CKC, 256, y[:, COL_CKC:COL_CKC + 256])
    put(COL_CKS, 512, y[:, COL_CKS:COL_CKS + 512])
    for g in range(4):
        c0 = COL_CKC + g * HEAD_DIM
        ck_ref[g] = y[:, c0:c0 + HEAD_DIM]
    g_ref[...] = y[:, COL_CG:COL_CG + LANES]
    qt_ref[...] = (y[:, COL_CQ:COL_CQ + 512] * SCALE).T.astype(BF16)
    vt = y[:, COL_CVS:COL_CVS + LANES].T
    ones_row = (lax.broadcasted_iota(jnp.int32, (HEAD_DIM, vt.shape[1]), 0) == 0).astype(F32)
    for kvh in range(2):
        vt_ref[kvh] = jnp.concatenate([vt[kvh * HEAD_DIM:(kvh + 1) * HEAD_DIM, :], ones_row], axis=0).astype(BF16)


def _inproj(x, mod_l, w_in_l, cos_t, sin_t, seq):
    tm = 512
    vec = lambda k: pl.BlockSpec((1, D_MODEL), lambda i, k=k: (0, k))
    return pl.pallas_call(
        _inproj_kernel,
        out_shape=(jax.ShapeDtypeStruct((seq, N_QKV), BF16),
                   jax.ShapeDtypeStruct((4, seq, HEAD_DIM), F32),
                   jax.ShapeDtypeStruct((seq, LANES), F32),
                   jax.ShapeDtypeStruct((512, seq), BF16),
                   jax.ShapeDtypeStruct((2, LANES, seq), BF16)),
        grid=(seq // tm,),
        in_specs=[
            pl.BlockSpec((tm, D_MODEL), lambda i: (i, 0)),
            vec(1), vec(0),
            pl.BlockSpec((D_MODEL, N_IN_PAD), lambda i: (0, 0)),
            pl.BlockSpec((tm, LANES), lambda i: (i, 0)),
            pl.BlockSpec((tm, LANES), lambda i: (i, 0)),
        ],
        out_specs=(pl.BlockSpec((tm, N_QKV), lambda i: (i, 0)),
                   pl.BlockSpec((4, tm, HEAD_DIM), lambda i: (0, i, 0)),
                   pl.BlockSpec((tm, LANES), lambda i: (i, 0)),
                   pl.BlockSpec((512, tm), lambda i: (0, i)),
                   pl.BlockSpec((2, LANES, tm), lambda i: (0, 0, i))),
        compiler_params=_cparams(("arbitrary",)),
        name="in_proj",
    )(x, mod_l, mod_l, w_in_l, cos_t, sin_t)


def _lane_half():
    return lax.broadcasted_iota(jnp.int32, (BLOCK, LANES), 1) >> 6


def _align_q(q, half, x, ysel):
    qx = jnp.where(half == x, q, 0.0)
    if ysel is None:
        return qx.astype(BF16)
    qd = qx + pltpu.roll(qx, HEAD_DIM, axis=1)
    return jnp.where(ysel, qd, 0.0).astype(BF16)


def _spread_o(o, ysel):
    if ysel is None:
        return o
    ob = jnp.where(ysel, o, 0.0)
    return ob + pltpu.roll(ob, HEAD_DIM, axis=1)


def _banded_kernel(*refs, nprev, tq, max_dist, use_bias, use_sinks, want_lse, group_div, row_axis):
    refs = list(refs)
    sink_ref = refs.pop(0) if use_sinks else None
    q_ref, kp_ref, kc_ref, vp_ref, vc_ref = refs[:5]
    refs = refs[5:]
    bias_ref = refs.pop(0) if use_bias else None
    o_ref = refs.pop(0)
    lse_ref = refs.pop(0) if want_lse else None
    kctx, vctx = refs

    tp = nprev * BLOCK
    ctx = (nprev + 1) * BLOCK
    i = pl.program_id(row_axis)
    p = pl.program_id(row_axis - 1)
    kctx[0:tp, :] = kp_ref[...]
    kctx[tp:tp + tq, :] = kc_ref[...]
    vctx[0:tp, :] = vp_ref[...]
    vctx[tp:tp + tq, :] = vc_ref[...]

    half = _lane_half()
    upper = half == 1
    ysel = None if group_div is None else half == p // group_div
    rr = lax.broadcasted_iota(jnp.int32, (BLOCK, ctx), 0)
    cc = lax.broadcasted_iota(jnp.int32, (BLOCK, ctx), 1)
    if not use_bias:
        dist = rr + tp - cc
        band = (dist >= 0) & (dist <= max_dist)

    for sub in range(tq // BLOCK):
        rows = slice(sub * BLOCK, (sub + 1) * BLOCK)
        q = q_ref[rows, :].astype(F32)
        k = kctx[sub * BLOCK:sub * BLOCK + ctx, :]
        v = vctx[sub * BLOCK:sub * BLOCK + ctx, :]
        low = tp - (i * tq + sub * BLOCK)
        inseq = cc >= low
        outs, lses = [], []
        for x in range(2):
            s = _dot_nt(_align_q(q, half, x, ysel), k)
            if use_bias:
                s = s + bias_ref[x]
            else:
                s = jnp.where(band, s, NEG)
            s = jnp.where(inseq, s, NEG)
            m = jnp.max(s, axis=1, keepdims=True)
            if use_sinks:
                sk = sink_ref[2 * p + x]
                m = jnp.maximum(m, sk)
            e = jnp.exp(s - m)
            den = jnp.sum(e, axis=1, keepdims=True)
            if use_sinks:
                den = den + jnp.exp(sk - m)
            o = _dot(e.astype(BF16), v) / den
            outs.append(_spread_o(o, ysel))
            lses.append(m + jnp.log(den))
        o_ref[rows, :] = jnp.where(upper, outs[1], outs[0])
        if want_lse:
            lse_ref[rows, :] = jnp.where(upper, jnp.broadcast_to(lses[1], (BLOCK, LANES)),
                                         jnp.broadcast_to(lses[0], (BLOCK, LANES)))


def _banded_call(qkv_view, nrows, nsub, row_width_blocks, q_cb, k_cb, v_cb, n_qblocks, *, nprev, tq,
                 max_dist, bias=None, sinks=None, want_lse=False, group_div=None, kv_per_qblock, name):
    tp = nprev * BLOCK
    ratio = tq // tp
    ctx = (nprev + 1) * BLOCK

    def kvb(p):
        return p if kv_per_qblock else 0

    in_specs = []
    args = []
    if sinks is not None:
        in_specs.append(pl.BlockSpec(memory_space=pltpu.SMEM))
        args.append(sinks)
    in_specs += [
        pl.BlockSpec((tq, LANES), lambda r, p, i: (i, r * row_width_blocks + q_cb + p)),
        pl.BlockSpec((tp, LANES), lambda r, p, i: (jnp.maximum(i * ratio - 1, 0), r * row_width_blocks + k_cb + kvb(p))),
        pl.BlockSpec((tq, LANES), lambda r, p, i: (i, r * row_width_blocks + k_cb + kvb(p))),
        pl.BlockSpec((tp, LANES), lambda r, p, i: (jnp.maximum(i * ratio - 1, 0), r * row_width_blocks + v_cb + kvb(p))),
        pl.BlockSpec((tq, LANES), lambda r, p, i: (i, r * row_width_blocks + v_cb + kvb(p))),
    ]
    args += [qkv_view] * 5
    if bias is not None:
        in_specs.append(pl.BlockSpec((2, BLOCK, ctx), lambda r, p, i: (p, 0, 0)))
        args.append(bias)
    out_w = nsub * n_qblocks * LANES
    o_spec = pl.BlockSpec((tq, LANES), lambda r, p, i: (i, r * n_qblocks + p))
    out_shape = [jax.ShapeDtypeStruct((nrows, out_w), F32)]
    out_specs = [o_spec]
    if want_lse:
        out_shape.append(jax.ShapeDtypeStruct((nrows, out_w), F32))
        out_specs.append(o_spec)
    kern = functools.partial(_banded_kernel, nprev=nprev, tq=tq, max_dist=max_dist, use_bias=bias is not None,
                             use_sinks=sinks is not None, want_lse=want_lse, group_div=group_div, row_axis=2)
    return pl.pallas_call(
        kern,
        out_shape=tuple(out_shape),
        grid=(nsub, n_qblocks, nrows // tq),
        in_specs=in_specs,
        out_specs=tuple(out_specs),
        scratch_shapes=[pltpu.VMEM((tp + tq, LANES), BF16), pltpu.VMEM((tp + tq, LANES), BF16)],
        compiler_params=_cparams(("arbitrary", "arbitrary", "arbitrary")),
        name=name,
    )(*args)


def _compress_kernel(x_ref, pos_ref, w1_ref, w2_ref, o_ref):
    x = x_ref[...]
    half = CMP_STRIDE * HEAD_DIM
    xa = (x + pos_ref[0:1, :]).astype(BF16)
    xb = (x + pos_ref[1:2, :]).astype(BF16)
    first = _dot(xa, w1_ref[0:half, :])
    second = _dot(xb, w1_ref[half:2 * half, :])
    n = x.shape[0]
    hid = first + pltpu.roll(second, n - 1, axis=0)
    act = jax.nn.gelu(hid)
    o_ref[...] = _dot(act.astype(BF16), w2_ref[...]).astype(BF16)


def _compress(ck, cmp_pos_l, w1_l, w2_l, seq):
    n_chunk = seq // CMP_STRIDE
    feat = CMP_STRIDE * HEAD_DIM
    x = ck.reshape(4, n_chunk, feat)
    pos = cmp_pos_l.reshape(2, 2, feat)
    out = pl.pallas_call(
        _compress_kernel,
        out_shape=jax.ShapeDtypeStruct((4, n_chunk, HEAD_DIM), BF16),
        grid=(4,),
        in_specs=[
            pl.BlockSpec((None, n_chunk, feat), lambda g: (g, 0, 0)),
            pl.BlockSpec((None, 2, feat), lambda g: (g // 2, 0, 0)),
            pl.BlockSpec((None, 2 * feat, CMP_HIDDEN), lambda g: (g // 2, 0, 0)),
            pl.BlockSpec((None, CMP_HIDDEN, HEAD_DIM), lambda g: (g // 2, 0, 0)),
        ],
        out_specs=pl.BlockSpec((None, n_chunk, HEAD_DIM), lambda g: (g, 0, 0)),
        compiler_params=_cparams(("arbitrary",)),
        name="compress_tokens",
    )(x, pos, w1_l, w2_l)
    kc = jnp.concatenate([out[0], out[1]], axis=1)
    vc = jnp.concatenate([out[2], out[3]], axis=1)
    return kc, vc


def _cmp_attn_kernel(q_ref, kc_ref, vc_ref, mt_ref, o_ref, mq_ref, *, n_cmp, n_blk):
    kvh = pl.program_id(0)
    qi = pl.program_id(1)
    half = _lane_half()
    upper = half == 1
    ysel = half == kvh
    qpos = qi * BLOCK + lax.broadcasted_iota(jnp.int32, (BLOCK, n_cmp), 0)
    cend = lax.broadcasted_iota(jnp.int32, (BLOCK, n_cmp), 1) * CMP_STRIDE + (CMP_BLOCK - 1)
    valid = cend <= qpos
    kc = kc_ref[...]
    vc = vc_ref[...]
    imp = jnp.zeros((BLOCK, n_cmp), F32)
    outs = []
    for h in range(4):
        j, x = divmod(h, 2)
        q = q_ref[:, j * LANES:(j + 1) * LANES].astype(F32)
        s = _dot_nt(_align_q(q, half, x, ysel), kc)
        s = jnp.where(valid, s, NEG)
        m = jnp.max(s, axis=1, keepdims=True)
        e = jnp.where(valid, jnp.exp(s - m), 0.0)
        prob = e / jnp.maximum(jnp.sum(e, axis=1, keepdims=True), 1e-30)
        imp = imp + prob
        outs.append(_spread_o(_dot(prob.astype(BF16), vc), ysel))
    o_ref[:, 0:LANES] = jnp.where(upper, outs[1], outs[0])
    o_ref[:, LANES:2 * LANES] = jnp.where(upper, outs[3], outs[2])

    hi = imp.astype(BF16)
    r1 = imp - hi.astype(F32)
    mid = r1.astype(BF16)
    lo = (r1 - mid.astype(F32)).astype(BF16)
    mt = mt_ref[...]
    imp_sel = _dot_nt(mt, hi) + _dot_nt(mt, mid) + _dot_nt(mt, lo)
    blk = lax.broadcasted_iota(jnp.int32, (n_blk, BLOCK), 0)
    cur = (qi * BLOCK + lax.broadcasted_iota(jnp.int32, (n_blk, BLOCK), 1)) >> 6
    forced = (blk == 0) | (blk == cur) | (blk == cur - 1)
    causal = blk <= cur
    score = jnp.where(forced, FORCE, jnp.where(causal, imp_sel, NEG))
    blk_f = blk.astype(F32)
    chosen = jnp.zeros((n_blk, BLOCK), F32)
    for _ in range(SEL_TOPK):
        mx = jnp.max(score, axis=0, keepdims=True)
        first = jnp.min(jnp.where(score == mx, blk_f, 1e9), axis=0, keepdims=True)
        pick = blk_f == first
        chosen = jnp.where(pick, 1.0, chosen)
        score = jnp.where(pick, -jnp.inf, score)
    mq_ref[...] = jnp.where(causal & (chosen > 0.5), 0.0, NEG).astype(BF16)


def _importance_matrix(n_blk, n_cmp):
    mt = np.zeros((n_blk, n_cmp), np.float32)
    ratio = SEL_BLOCK // CMP_STRIDE
    for o in range(-(CMP_BLOCK // CMP_STRIDE - 1), ratio):
        w = max(0, min(SEL_BLOCK, o * CMP_STRIDE + CMP_BLOCK) - max(0, o * CMP_STRIDE)) / CMP_BLOCK
        for j in range(n_blk):
            c = j * ratio + o
            if 0 <= c < n_cmp and j * SEL_BLOCK < n_cmp * CMP_STRIDE:
                mt[j, c] = w
    return jnp.asarray(mt, dtype=BF16)


def _cmp_attn(qkv, kc, vc, seq):
    n_cmp = seq // CMP_STRIDE
    n_blk = max(LANES, seq // SEL_BLOCK)
    mt = _importance_matrix(n_blk, n_cmp)
    kern = functools.partial(_cmp_attn_kernel, n_cmp=n_cmp, n_blk=n_blk)
    return pl.pallas_call(
        kern,
        out_shape=(jax.ShapeDtypeStruct((seq, 512), F32),
                   jax.ShapeDtypeStruct((2, seq // BLOCK, n_blk, BLOCK), BF16)),
        grid=(2, seq // BLOCK),
        in_specs=[
            pl.BlockSpec((BLOCK, 256), lambda h, i: (i, COL_CQ // 256 + h)),
            pl.BlockSpec((n_cmp, LANES), lambda h, i: (0, 0)),
            pl.BlockSpec((n_cmp, LANES), lambda h, i: (0, 0)),
            pl.BlockSpec((n_blk, n_cmp), lambda h, i: (0, 0)),
        ],
        out_specs=(pl.BlockSpec((BLOCK, 256), lambda h, i: (i, h)),
                   pl.BlockSpec((None, None, n_blk, BLOCK), lambda h, i: (h, i, 0, 0))),
        compiler_params=_cparams(("arbitrary", "arbitrary")),
        name="cmp_attn_topk",
    )(qkv, kc, vc, mt)


def _sel_kernel(qi_ref, kt_ref, qt_ref, mq_ref, k_ref, vt_ref, e_ref, tab_ref, b31_ref, o_ref,
                qaug, m_s, acc_s, *, n_e):
    kvh = pl.program_id(0)
    step = pl.program_id(1)
    qi = qi_ref[step]
    kt = kt_ref[step]
    q_per_sub = SEL_SUB // SEL_TQ
    win_steps = LANES * SEL_BLOCK // SEL_TK
    ncol = 4 * SEL_TQ

    @pl.when(kt == 0)
    def _():
        m_s[...] = jnp.full(m_s.shape, -3e38, F32)
        acc_s[...] = jnp.zeros(acc_s.shape, F32)

    @pl.when(kt % win_steps == 0)
    def _():
        row_half = lax.broadcasted_iota(jnp.int32, (LANES, SEL_TQ), 0) >> 6
        mq = mq_ref[...]
        for h in range(4):
            qh = qt_ref[h * HEAD_DIM:(h + 1) * HEAD_DIM, :].astype(F32)
            both = jnp.concatenate([qh, qh], axis=0)
            qaug[0:LANES, h * SEL_TQ:(h + 1) * SEL_TQ] = jnp.where(row_half == kvh, both, 0.0).astype(BF16)
            qaug[LANES:2 * LANES, h * SEL_TQ:(h + 1) * SEL_TQ] = mq

    sub_near0 = jnp.maximum(0, (qi - SEL_SLAB_HI + q_per_sub - 1) // q_per_sub)

    def run(with_bias):
        for pair in range(2):
            cols = slice(pair * 2 * SEL_TQ, (pair + 1) * 2 * SEL_TQ)
            qa = qaug[:, cols]
            if with_bias:
                b31 = jnp.concatenate([jnp.full((1, SEL_TQ), b31_ref[kvh * 4 + pair * 2 + hh], F32)
                                       for hh in range(2)], axis=1)
            m_locs, pvs = [], []
            for j in range(SEL_NSUB):
                sub = kt * SEL_NSUB + j
                is_far = sub < sub_near0
                for c in range(SEL_SUB // SEL_CHUNK):
                    r0 = j * SEL_SUB + c * SEL_CHUNK
                    kaug = jnp.concatenate([k_ref[r0:r0 + SEL_CHUNK, :],
                                            e_ref[sub % n_e, c * SEL_CHUNK:(c + 1) * SEL_CHUNK, :]], axis=1)
                    s = _dot(kaug, qa)
                    if with_bias:
                        blocks = []
                        for hh in range(2):
                            col = []
                            for tt in range(SEL_CHUNK // BLOCK):
                                t = c * (SEL_CHUNK // BLOCK) + tt
                                u = jnp.maximum(qi - q_per_sub * sub - t, SEL_SLAB_LO)
                                uidx = jnp.where(is_far, SEL_NSLAB - 1, u - SEL_SLAB_LO)
                                col.append(s[tt * BLOCK:(tt + 1) * BLOCK, hh * SEL_TQ:(hh + 1) * SEL_TQ]
                                           + tab_ref[pair * 2 + hh, uidx])
                            blocks.append(jnp.concatenate(col, axis=0))
                        s = jnp.concatenate(blocks, axis=1)
                    m_loc = jnp.max(s, axis=0, keepdims=True)
                    pr = jnp.exp(s - m_loc).astype(BF16)
                    pvs.append(_dot(vt_ref[:, r0:r0 + SEL_CHUNK], pr))
                    if with_bias:
                        m_loc = m_loc + jnp.where(is_far, b31, 0.0)
                    m_locs.append(m_loc)
            m_run = m_s[:, cols]
            if with_bias:
                m_run = m_run + jnp.where(sub_near0 >= kt * SEL_NSUB, b31, 0.0)
            m_new = m_run
            for m_loc in m_locs:
                m_new = jnp.maximum(m_new, m_loc)
            acc = jnp.exp(m_run - m_new) * acc_s[:, cols]
            for m_loc, pv in zip(m_locs, pvs):
                acc = acc + jnp.exp(m_loc - m_new) * pv
            acc_s[:, cols] = acc
            m_s[:, cols] = m_new

    all_far = kt * SEL_NSUB + SEL_NSUB - 1 < sub_near0

    @pl.when(all_far)
    def _():
        run(False)

    @pl.when(jnp.logical_not(all_far))
    def _():
        run(True)

    @pl.when(kt == qi // (SEL_TK // SEL_TQ))
    def _():
        acc = acc_s[...]
        o = acc[0:HEAD_DIM, :] / acc[HEAD_DIM:HEAD_DIM + 1, :]
        for jp in range(2):
            blk = jnp.concatenate([o[:, (2 * jp) * SEL_TQ:(2 * jp + 1) * SEL_TQ],
                                   o[:, (2 * jp + 1) * SEL_TQ:(2 * jp + 2) * SEL_TQ]], axis=0)
            o_ref[:, jp * LANES:(jp + 1) * LANES] = blk.T


def _sel_attn(qkv, qt, vt, mq, bias_s, rel_bias, seq):
    n_q = seq // SEL_TQ
    per_step = SEL_TK // SEL_TQ
    qi_l, kt_l = [], []
    for qi in range(n_q):
        for kt in range(qi // per_step + 1):
            qi_l.append(qi)
            kt_l.append(kt)
    qi_arr = jnp.asarray(np.array(qi_l, np.int32))
    kt_arr = jnp.asarray(np.array(kt_l, np.int32))
    win_steps = LANES * SEL_BLOCK // SEL_TK
    n_e = min(LANES * SEL_BLOCK // SEL_SUB, seq // SEL_SUB)
    e_np = np.zeros((n_e, SEL_SUB, LANES), np.float32)
    for t in range(n_e):
        for k in range(SEL_SUB):
            e_np[t, k, (t * (SEL_SUB // SEL_BLOCK) + k // SEL_BLOCK) % LANES] = 1.0
    e_all = jnp.asarray(e_np, dtype=BF16)
    b31 = rel_bias[REL_BUCKETS - 1, 4:12]
    grid_spec = pltpu.PrefetchScalarGridSpec(
        num_scalar_prefetch=2,
        grid=(2, len(qi_l)),
        in_specs=[
            pl.BlockSpec((4 * HEAD_DIM, SEL_TQ), lambda h, s, qi, kt: (h, qi[s])),
            pl.BlockSpec((None, None, LANES, SEL_TQ), lambda h, s, qi, kt: (h, qi[s], kt[s] // win_steps, 0)),
            pl.BlockSpec((SEL_TK, LANES), lambda h, s, qi, kt: (kt[s], COL_CKS // LANES)),
            pl.BlockSpec((None, LANES, SEL_TK), lambda h, s, qi, kt: (h, 0, kt[s])),
            pl.BlockSpec((n_e, SEL_SUB, LANES), lambda h, s, qi, kt: (0, 0, 0)),
            pl.BlockSpec((4, SEL_NSLAB, BLOCK, BLOCK), lambda h, s, qi, kt: (h, 0, 0, 0)),
            pl.BlockSpec(memory_space=pltpu.SMEM),
        ],
        out_specs=pl.BlockSpec((SEL_TQ, 256), lambda h, s, qi, kt: (qi[s], h)),
        scratch_shapes=[pltpu.VMEM((2 * LANES, 4 * SEL_TQ), BF16),
                        pltpu.VMEM((1, 4 * SEL_TQ), F32),
                        pltpu.VMEM((LANES, 4 * SEL_TQ), F32)],
    )
    return pl.pallas_call(
        functools.partial(_sel_kernel, n_e=n_e),
        out_shape=jax.ShapeDtypeStruct((seq, 512), F32),
        grid_spec=grid_spec,
        compiler_params=_cparams(("arbitrary", "arbitrary")),
        name="selected_attn",
    )(qi_arr, kt_arr, qt, mq, qkv, vt, e_all, bias_s, b31)


def _layer_norm(z, g, b):
    mu = jnp.mean(z, axis=-1, keepdims=True)
    zc = z - mu
    var = jnp.mean(jnp.square(zc), axis=-1, keepdims=True)
    return zc * lax.rsqrt(var + LN_EPS) * g + b


def _outproj_kernel(x_ref, oa0, oa1, oa2, la0, la1, la2, ob_ref, ocmp_ref, oslc_ref, owin_ref, g_ref,
                    gx_ref, w_ref, ga_ref, lng_ref, lnb_ref, o_ref):
    l0, l1, l2 = la0[...], la1[...], la2[...]
    mx = jnp.maximum(jnp.maximum(l0, l1), l2)
    w0, w1, w2 = jnp.exp(l0 - mx), jnp.exp(l1 - mx), jnp.exp(l2 - mx)
    oa = (w0 * oa0[...] + w1 * oa1[...] + w2 * oa2[...]) / (w0 + w1 + w2)
    gt = jax.nn.sigmoid(g_ref[...])
    hi = gt.astype(BF16)
    lo = (gt - hi.astype(F32)).astype(BF16)
    gx = gx_ref[...]
    gates = _dot(hi, gx) + _dot(lo, gx)
    oc = gates[:, 0:512] * ocmp_ref[...] + gates[:, 512:1024] * oslc_ref[...] + gates[:, 1024:1536] * owin_ref[...]
    mixed = jnp.concatenate([oa, ob_ref[...], oc], axis=1).astype(BF16)
    y = _dot(mixed, w_ref[...])
    z = DN_ALPHA * x_ref[...] + (1 + ga_ref[...]) * y
    o_ref[...] = _layer_norm(z, lng_ref[...], lnb_ref[...])


def _gate_expand():
    gx = np.zeros((LANES, 3 * 512), np.float32)
    for h in range(8):
        for b in range(3):
            gx[h * 3 + b, b * 512 + h * HEAD_DIM:b * 512 + (h + 1) * HEAD_DIM] = 1.0
    return jnp.asarray(gx, dtype=BF16)


def _outproj(x, oa, la, ob, ocmp, oslc, owin, gates, w_out_l, mod_l, lng, lnb, seq):
    tm = 256
    row = lambda w: pl.BlockSpec((tm, w), lambda i: (i, 0))
    vec = lambda k: pl.BlockSpec((1, D_MODEL), lambda i, k=k: (0, k))
    return pl.pallas_call(
        _outproj_kernel,
        out_shape=jax.ShapeDtypeStruct((seq, D_MODEL), F32),
        grid=(seq // tm,),
        in_specs=[row(D_MODEL)] + [row(256)] * 7 + [row(512)] * 3 + [row(LANES)] + [
            pl.BlockSpec((LANES, 3 * 512), lambda i: (0, 0)),
            pl.BlockSpec((D_MODEL, D_MODEL), lambda i: (0, 0)),
            vec(2), vec(0), vec(0)],
        out_specs=row(D_MODEL),
        compiler_params=_cparams(("arbitrary",)),
        name="out_proj_ln",
    )(x, oa[0], oa[1], oa[2], la[0], la[1], la[2], ob, ocmp, oslc, owin, gates, _gate_expand(), w_out_l,
      mod_l, lng, lnb)


def _mlp_kernel(x_ref, sc_ref, sh_ref, g_ref, w1_ref, w2_ref, lng_ref, lnb_ref, o_ref, h_s, acc_s):
    j = pl.program_id(1)

    @pl.when(j == 0)
    def _():
        h_s[...] = (x_ref[...] * (1 + sc_ref[...]) + sh_ref[...]).astype(BF16)
        acc_s[...] = jnp.zeros(acc_s.shape, F32)

    f = jnp.maximum(_dot(h_s[...], w1_ref[...]), 0.0)
    acc_s[...] += _dot(jnp.square(f).astype(BF16), w2_ref[...])

    @pl.when(j == pl.num_programs(1) - 1)
    def _():
        z = DN_ALPHA * x_ref[...] + (1 + g_ref[...]) * acc_s[...]
        o_ref[...] = _layer_norm(z, lng_ref[...], lnb_ref[...])


def _mlp(x, mod_l, w1_l, w2_l, lng, lnb, seq):
    tm = 1024
    tf = 1024
    vec = lambda k: pl.BlockSpec((1, D_MODEL), lambda i, j, k=k: (0, k))
    return pl.pallas_call(
        _mlp_kernel,
        out_shape=jax.ShapeDtypeStruct((seq, D_MODEL), F32),
        grid=(seq // tm, D_FF // tf),
        in_specs=[
            pl.BlockSpec((tm, D_MODEL), lambda i, j: (i, 0)),
            vec(4), vec(3), vec(5),
            pl.BlockSpec((D_MODEL, tf), lambda i, j: (0, j)),
            pl.BlockSpec((tf, D_MODEL), lambda i, j: (j, 0)),
            vec(0), vec(0),
        ],
        out_specs=pl.BlockSpec((tm, D_MODEL), lambda i, j: (i, 0)),
        scratch_shapes=[pltpu.VMEM((tm, D_MODEL), BF16), pltpu.VMEM((tm, D_MODEL), F32)],
        compiler_params=_cparams(("arbitrary", "arbitrary")),
        name="mlp_ln",
    )(x, mod_l, mod_l, mod_l, w1_l, w2_l, lng, lnb)


def kernel(x, c, positions, w_in, w_out, rel_bias, sinks, cmp_pos, cmp_w1, cmp_w2, mlp_w1, mlp_w2,
           ada_w, ada_b, ln_g, ln_b):
    bsz, seq, d = x.shape
    assert bsz == 1 and d == D_MODEL and seq % (2 * A_CONFIGS[-1][0]) == 0
    x = x.reshape(seq, D_MODEL)

    mod = _modulation(c, ada_w, ada_b)
    cos_t, sin_t = _rope_tables(positions, seq)
    bias_a, bias_w, bias_s = _bias_tables(rel_bias)

    w_in_b = jnp.pad(w_in, ((0, 0), (0, 0), (0, N_IN_PAD - N_IN))).astype(BF16)
    w_out_b = w_out.astype(BF16)
    w1_b = mlp_w1.astype(BF16)
    w2_b = mlp_w2.astype(BF16)
    cw1_b = cmp_w1.astype(BF16)
    cw2_b = cmp_w2.astype(BF16)

    for l in range(DEPTH):
        mod_l = mod[l]
        qkv, ck, gates, qt, vt = _inproj(x, mod_l, w_in_b[l], cos_t, sin_t, seq)

        oa, la = [], []
        for ci, (window, dil) in enumerate(A_CONFIGS):
            nrows = seq // dil
            view = qkv.reshape(nrows, dil * N_QKV)
            o_c, l_c = _banded_call(
                view, nrows, dil, N_QKV // LANES, COL_AQ // LANES, COL_AK // LANES, COL_AV // LANES, 2,
                nprev=1, tq=256, max_dist=window // dil, bias=bias_a[ci], want_lse=True,
                group_div=None, kv_per_qblock=True, name=f"dilated_attn_{dil}")
            oa.append(o_c.reshape(seq, 256))
            la.append(l_c.reshape(seq, 256))

        (ob,) = _banded_call(
            qkv, seq, 1, N_QKV // LANES, COL_BQ // LANES, COL_BK // LANES, COL_BV // LANES, 2,
            nprev=1, tq=256, max_dist=B_WINDOW - 1, sinks=sinks[l], group_div=1,
            kv_per_qblock=False, name="swa_attn")

        kc, vc = _compress(ck, cmp_pos[l], cw1_b[l], cw2_b[l], seq)
        ocmp, mq = _cmp_attn(qkv, kc, vc, seq)
        oslc = _sel_attn(qkv, qt, vt, mq, bias_s, rel_bias, seq)
        (owin,) = _banded_call(
            qkv, seq, 1, N_QKV // LANES, COL_CQ // LANES, COL_CKW // LANES, COL_CVW // LANES, 4,
            nprev=C_WINDOW // BLOCK, tq=C_WINDOW, max_dist=C_WINDOW - 1, bias=bias_w, group_div=2,
            kv_per_qblock=False, name="window_attn")

        x = _outproj(x, oa, la, ob, ocmp, oslc, owin, gates, w_out_b[l], mod_l,
                     ln_g[l, 0:1], ln_b[l, 0:1], seq)
        x = _mlp(x, mod_l, w1_b[l], w2_b[l], ln_g[l, 1:2], ln_b[l, 1:2], seq)

    return x.reshape(bsz, seq, D_MODEL)
```

```python
import functools
import math

import numpy as np
import jax
import jax.numpy as jnp
from jax import lax
from jax.experimental import pallas as pl
from jax.experimental.pallas import tpu as pltpu

F32 = jnp.float32
BF16 = jnp.bfloat16

D_MODEL = 1024
DEPTH = 4
HEAD_DIM = 64
LANES = 128
BLOCK = 128
A_CONFIGS = ((128, 1), (512, 4), (2048, 16))
B_WINDOW = 128
ROPE_THETA = 150000.0
CMP_BLOCK = 32
CMP_STRIDE = 16
CMP_HIDDEN = 256
SEL_BLOCK = 64
SEL_TOPK = 16
C_WINDOW = 512
REL_BUCKETS = 32
REL_MAX_DIST = 2048
D_FF = 4 * D_MODEL
DN_ALPHA = (2 * DEPTH) ** 0.25
LN_EPS = 1e-5
NEG = -1e30
FORCE = 1e4
SCALE = HEAD_DIM ** -0.5

COL_AQ, COL_AK, COL_AV = 0, 256, 512
COL_BQ, COL_BK, COL_BV = 768, 1024, 1152
COL_CQ = 1280
COL_CKC = 1792
COL_CKS, COL_CVS = 2048, 2176
COL_CKW, COL_CVW = 2304, 2432
COL_CG = 2560
N_A = 768
N_QKV = 1536


def _qkv_col(col):
    assert COL_BQ <= col < COL_CKC or COL_CKS <= col < COL_CG
    return col - N_A if col < COL_CKC else col - N_A - 256
N_IN = 2584
N_IN_PAD = 2688

BAND_TQ = 1024
SEL_TQ = 128
SEL_SUB = 512
SEL_CHUNK = 128
SEL_VROWS = LANES
SEL_NSUB = 4
SEL_TK = SEL_SUB * SEL_NSUB
SEL_SLAB_LO = -1
SEL_SLAB_HI = 15
SEL_NSLAB = SEL_SLAB_HI - SEL_SLAB_LO + 2
VMEM_LIMIT = 56 * 1024 * 1024


def _cparams(sem, flags=None):
    return pltpu.CompilerParams(dimension_semantics=sem, vmem_limit_bytes=VMEM_LIMIT, flags=flags)


def _dot_nt(a, b):
    return lax.dot_general(a, b, (((1,), (1,)), ((), ())), preferred_element_type=F32)


def _dot(a, b):
    return jnp.dot(a, b, preferred_element_type=F32)


def _mod_kernel(c_ref, w_ref, b_ref, o_ref):
    c = c_ref[...]
    act = c * jax.nn.sigmoid(c)
    lhs = jnp.broadcast_to(act, (8, D_MODEL)).astype(BF16)
    y = _dot(lhs, w_ref[...].astype(BF16))
    o_ref[...] = y[0:1, :] + b_ref[...]


def _modulation(c, ada_w, ada_b):
    tn = 1536
    return pl.pallas_call(
        _mod_kernel,
        out_shape=jax.ShapeDtypeStruct((DEPTH, 1, 6 * D_MODEL), F32),
        grid=(DEPTH, 6 * D_MODEL // tn),
        in_specs=[
            pl.BlockSpec((1, D_MODEL), lambda l, j: (0, 0)),
            pl.BlockSpec((None, D_MODEL, tn), lambda l, j: (l, 0, j)),
            pl.BlockSpec((None, 1, tn), lambda l, j: (l, 0, j)),
        ],
        out_specs=pl.BlockSpec((None, 1, tn), lambda l, j: (l, 0, j)),
        compiler_params=_cparams(("arbitrary", "arbitrary")),
        name="adaln_mod",
    )(c, ada_w, ada_b.reshape(DEPTH, 1, 6 * D_MODEL))


def _rope_table_kernel(pos_ref, freq_ref, cos_ref, sin_ref):
    ang = pos_ref[...].astype(F32) * freq_ref[...]
    cos_ref[...] = jnp.cos(ang)
    sin_ref[...] = jnp.sin(ang)


def _rope_tables(positions, seq):
    half = HEAD_DIM // 2
    freq = ROPE_THETA ** (-jnp.arange(half, dtype=F32) / half)
    freq = jnp.tile(freq, LANES // half).reshape(1, LANES)
    tm = 1024
    return pl.pallas_call(
        _rope_table_kernel,
        out_shape=(jax.ShapeDtypeStruct((seq, LANES), F32),) * 2,
        grid=(seq // tm,),
        in_specs=[pl.BlockSpec((tm, 1), lambda i: (i, 0)), pl.BlockSpec((1, LANES), lambda i: (0, 0))],
        out_specs=(pl.BlockSpec((tm, LANES), lambda i: (i, 0)),) * 2,
        compiler_params=_cparams(("arbitrary",)),
        name="rope_tables",
    )(positions.reshape(seq, 1), freq)


def _t5_bias(rb_ref, dist, col, max_dist):
    exact = REL_BUCKETS // 2
    d = jnp.maximum(dist, 0)
    df = jnp.maximum(d, 1).astype(F32)
    large = exact + (jnp.log(df / exact) / math.log(REL_MAX_DIST / exact)
                     * (REL_BUCKETS - exact)).astype(jnp.int32)
    large = jnp.minimum(large, REL_BUCKETS - 1)
    bucket = jnp.where(d < exact, d, large)
    out = jnp.zeros(dist.shape, F32)
    for b in range(REL_BUCKETS):
        out = jnp.where(bucket == b, rb_ref[b * 12 + col], out)
    return out, bucket


def _bias_a_kernel(rb_ref, o_ref):
    cfg = pl.program_id(0)
    head = pl.program_id(1)
    dil = jnp.where(cfg == 0, 1, jnp.where(cfg == 1, 4, 16))
    r = lax.broadcasted_iota(jnp.int32, (BLOCK, 2 * BLOCK), 0)
    c = lax.broadcasted_iota(jnp.int32, (BLOCK, 2 * BLOCK), 1)
    dist = r + BLOCK - c
    bias, _ = _t5_bias(rb_ref, dist * dil, head, None)
    o_ref[...] = jnp.where((dist >= 0) & (dist <= BLOCK), bias, NEG)


def _bias_w_kernel(rb_ref, o_ref):
    head = pl.program_id(0)
    nprev = C_WINDOW // BLOCK
    r = lax.broadcasted_iota(jnp.int32, (BLOCK, (nprev + 1) * BLOCK), 0)
    c = lax.broadcasted_iota(jnp.int32, (BLOCK, (nprev + 1) * BLOCK), 1)
    dist = r + nprev * BLOCK - c
    bias, _ = _t5_bias(rb_ref, dist, 4 + head, None)
    o_ref[...] = jnp.where((dist >= 0) & (dist <= C_WINDOW - 1), bias, NEG)


def _bias_s_kernel(rb_ref, o_ref):
    head = pl.program_id(0)
    slab = pl.program_id(1)
    r = lax.broadcasted_iota(jnp.int32, (BLOCK, BLOCK), 1)
    c = lax.broadcasted_iota(jnp.int32, (BLOCK, BLOCK), 0)
    dist = (slab + SEL_SLAB_LO) * BLOCK + r - c
    bias, _ = _t5_bias(rb_ref, dist, 4 + head, None)
    val = jnp.where(dist >= 0, bias, NEG)
    o_ref[...] = jnp.where(slab == SEL_NSLAB - 1, 0.0, val)


def _bias_tables(rel_bias):
    rb = rel_bias.reshape(-1)
    smem = pl.BlockSpec(memory_space=pltpu.SMEM)
    bias_a = pl.pallas_call(
        _bias_a_kernel,
        out_shape=jax.ShapeDtypeStruct((3, 4, BLOCK, 2 * BLOCK), F32),
        grid=(3, 4),
        in_specs=[smem],
        out_specs=pl.BlockSpec((None, None, BLOCK, 2 * BLOCK), lambda a, h: (a, h, 0, 0)),
        compiler_params=_cparams(("arbitrary", "arbitrary")),
        name="bias_table_dilated",
    )(rb)
    wctx = C_WINDOW + BLOCK
    bias_w = pl.pallas_call(
        _bias_w_kernel,
        out_shape=jax.ShapeDtypeStruct((8, BLOCK, wctx), F32),
        grid=(8,),
        in_specs=[smem],
        out_specs=pl.BlockSpec((None, BLOCK, wctx), lambda h: (h, 0, 0)),
        compiler_params=_cparams(("arbitrary",)),
        name="bias_table_window",
    )(rb)
    bias_s = pl.pallas_call(
        _bias_s_kernel,
        out_shape=jax.ShapeDtypeStruct((8, SEL_NSLAB, BLOCK, BLOCK), F32),
        grid=(8, SEL_NSLAB),
        in_specs=[smem],
        out_specs=pl.BlockSpec((None, None, BLOCK, BLOCK), lambda h, u: (h, u, 0, 0)),
        compiler_params=_cparams(("arbitrary", "arbitrary")),
        name="bias_table_selected",
    )(rb)
    return bias_a, bias_w, bias_s


def _rope_apply(t, cos, sin, first):
    up = pltpu.roll(t, LANES - HEAD_DIM // 2, axis=1)
    dn = pltpu.roll(t, HEAD_DIM // 2, axis=1)
    return t * cos + jnp.where(first, -up, dn) * sin


def _inproj_kernel(x_ref, sc_ref, sh_ref, w_ref, cos_ref, sin_ref, qkv_ref, ck_ref, g_ref, qt_ref, vt_ref,
                   a_ref):
    h = (x_ref[...] * (1 + sc_ref[...]) + sh_ref[...]).astype(BF16)
    y = _dot(h, w_ref[...])
    cos = cos_ref[...]
    sin = sin_ref[...]
    lane = lax.broadcasted_iota(jnp.int32, cos.shape, 1)
    first = (lane & (HEAD_DIM - 1)) < HEAD_DIM // 2

    def put(c0, width, val):
        q0 = _qkv_col(c0)
        qkv_ref[:, q0:q0 + width] = val.astype(BF16)

    a_ref[:, 0:256] = (y[:, COL_AQ:COL_AQ + 256] * SCALE).astype(BF16)
    a_ref[:, 256:768] = y[:, COL_AK:COL_AK + 512].astype(BF16)
    for j in range(2):
        c0 = COL_BQ + j * LANES
        put(c0, LANES, _rope_apply(y[:, c0:c0 + LANES], cos, sin, first) * SCALE)
    put(COL_BK, LANES, _rope_apply(y[:, COL_BK:COL_BK + LANES], cos, sin, first))
    put(COL_BV, LANES, y[:, COL_BV:COL_BV + LANES])
    put(COL_CQ, 512, y[:, COL_CQ:COL_CQ + 512] * SCALE)
    put(COL_CKS, 512, y[:, COL_CKS:COL_CKS + 512])
    for g in range(4):
        c0 = COL_CKC + g * HEAD_DIM
        ck_ref[g] = y[:, c0:c0 + HEAD_DIM]
    g_ref[...] = y[:, COL_CG:COL_CG + LANES]
    qt_ref[...] = (y[:, COL_CQ:COL_CQ + 512] * SCALE).T.astype(BF16)
    vt = y[:, COL_CVS:COL_CVS + LANES].T
    ones_row = (lax.broadcasted_iota(jnp.int32, (HEAD_DIM, vt.shape[1]), 0) == 0).astype(F32)
    for kvh in range(2):
        vt_ref[kvh] = jnp.concatenate([vt[kvh * HEAD_DIM:(kvh + 1) * HEAD_DIM, :], ones_row], axis=0).astype(BF16)


def _inproj(x, mod_l, w_in_l, cos_t, sin_t, seq):
    tm = 512
    vec = lambda k: pl.BlockSpec((1, D_MODEL), lambda i, k=k: (0, k))
    return pl.pallas_call(
        _inproj_kernel,
        out_shape=(jax.ShapeDtypeStruct((seq, N_QKV), BF16),
                   jax.ShapeDtypeStruct((4, seq, HEAD_DIM), F32),
                   jax.ShapeDtypeStruct((seq, LANES), F32),
                   jax.ShapeDtypeStruct((512, seq), BF16),
                   jax.ShapeDtypeStruct((2, LANES, seq), BF16),
                   jax.ShapeDtypeStruct((seq, N_A), BF16)),
        grid=(seq // tm,),
        in_specs=[
            pl.BlockSpec((tm, D_MODEL), lambda i: (i, 0)),
            vec(1), vec(0),
            pl.BlockSpec((D_MODEL, N_IN_PAD), lambda i: (0, 0)),
            pl.BlockSpec((tm, LANES), lambda i: (i, 0)),
            pl.BlockSpec((tm, LANES), lambda i: (i, 0)),
        ],
        out_specs=(pl.BlockSpec((tm, N_QKV), lambda i: (i, 0)),
                   pl.BlockSpec((4, tm, HEAD_DIM), lambda i: (0, i, 0)),
                   pl.BlockSpec((tm, LANES), lambda i: (i, 0)),
                   pl.BlockSpec((512, tm), lambda i: (0, i)),
                   pl.BlockSpec((2, LANES, tm), lambda i: (0, 0, i)),
                   pl.BlockSpec((tm, N_A), lambda i: (i, 0))),
        compiler_params=_cparams(("arbitrary",)),
        name="in_proj",
    )(x, mod_l, mod_l, w_in_l, cos_t, sin_t)


def _lane_half():
    return lax.broadcasted_iota(jnp.int32, (BLOCK, LANES), 1) >> 6


def _align_q(q, half, x, ysel):
    qx = jnp.where(half == x, q, 0.0)
    if ysel is None:
        return qx.astype(BF16)
    qd = qx + pltpu.roll(qx, HEAD_DIM, axis=1)
    return jnp.where(ysel, qd, 0.0).astype(BF16)


def _spread_o(o, ysel):
    if ysel is None:
        return o
    ob = jnp.where(ysel, o, 0.0)
    return ob + pltpu.roll(ob, HEAD_DIM, axis=1)


def _banded_kernel(*refs, nprev, tq, max_dist, use_bias, use_sinks, want_lse, group_div, row_axis):
    refs = list(refs)
    sink_ref = refs.pop(0) if use_sinks else None
    q_ref, kp_ref, kc_ref, vp_ref, vc_ref = refs[:5]
    refs = refs[5:]
    bias_ref = refs.pop(0) if use_bias else None
    o_ref = refs.pop(0)
    lse_ref = refs.pop(0) if want_lse else None
    kctx, vctx = refs

    tp = nprev * BLOCK
    ctx = (nprev + 1) * BLOCK
    i = pl.program_id(row_axis)
    p = pl.program_id(row_axis - 1)
    kctx[0:tp, :] = kp_ref[...]
    kctx[tp:tp + tq, :] = kc_ref[...]
    vctx[0:tp, :] = vp_ref[...]
    vctx[tp:tp + tq, :] = vc_ref[...]

    half = _lane_half()
    upper = half == 1
    ysel = None if group_div is None else half == p // group_div
    chunks = [(c0, min(c0 + 2 * BLOCK, ctx)) for c0 in range(0, ctx, 2 * BLOCK)]
    rrs, ccs, bands = [], [], []
    for c0, c1 in chunks:
        rr = lax.broadcasted_iota(jnp.int32, (BLOCK, c1 - c0), 0)
        cc = lax.broadcasted_iota(jnp.int32, (BLOCK, c1 - c0), 1) + c0
        ccs.append(cc)
        if not use_bias:
            dist = rr + tp - cc
            bands.append((dist >= 0) & (dist <= max_dist))

    for sub in range(tq // BLOCK):
        rows = slice(sub * BLOCK, (sub + 1) * BLOCK)
        q = q_ref[rows, :].astype(F32)
        low = tp - (i * tq + sub * BLOCK)
        outs, lses = [], []
        for x in range(2):
            qa = _align_q(q, half, x, ysel)
            ms, dens, pvs = [], [], []
            for ci, (c0, c1) in enumerate(chunks):
                k = kctx[sub * BLOCK + c0:sub * BLOCK + c1, :]
                v = vctx[sub * BLOCK + c0:sub * BLOCK + c1, :]
                s = _dot_nt(qa, k)
                if use_bias:
                    s = s + bias_ref[x, :, c0:c1]
                else:
                    s = jnp.where(bands[ci], s, NEG)
                s = jnp.where(ccs[ci] >= low, s, NEG)
                m_loc = jnp.max(s, axis=1, keepdims=True)
                e = jnp.exp(s - m_loc)
                ms.append(m_loc)
                dens.append(jnp.sum(e, axis=1, keepdims=True))
                pvs.append(_dot(e.astype(BF16), v))
            m = ms[0]
            for m_loc in ms[1:]:
                m = jnp.maximum(m, m_loc)
            if use_sinks:
                sk = sink_ref[2 * p + x]
                m = jnp.maximum(m, sk)
            ws = [jnp.exp(m_loc - m) for m_loc in ms]
            den = ws[0] * dens[0]
            acc = ws[0] * pvs[0]
            for w, d, pv in zip(ws[1:], dens[1:], pvs[1:]):
                den = den + w * d
                acc = acc + w * pv
            if use_sinks:
                den = den + jnp.exp(sk - m)
            o = acc / den
            outs.append(_spread_o(o, ysel))
            lses.append(m + jnp.log(den))
        o_ref[rows, :] = jnp.where(upper, outs[1], outs[0])
        if want_lse:
            lse_ref[rows, :] = jnp.where(upper, jnp.broadcast_to(lses[1], (BLOCK, LANES)),
                                         jnp.broadcast_to(lses[0], (BLOCK, LANES)))


def _banded_call(qkv_view, nrows, nsub, row_width_blocks, q_cb, k_cb, v_cb, n_qblocks, *, nprev, tq,
                 max_dist, bias=None, sinks=None, want_lse=False, group_div=None, kv_per_qblock, name):
    tp = nprev * BLOCK
    ratio = tq // tp
    ctx = (nprev + 1) * BLOCK

    def kvb(p):
        return p if kv_per_qblock else 0

    in_specs = []
    args = []
    if sinks is not None:
        in_specs.append(pl.BlockSpec(memory_space=pltpu.SMEM))
        args.append(sinks)
    in_specs += [
        pl.BlockSpec((tq, LANES), lambda r, p, i: (i, r * row_width_blocks + q_cb + p)),
        pl.BlockSpec((tp, LANES), lambda r, p, i: (jnp.maximum(i * ratio - 1, 0), r * row_width_blocks + k_cb + kvb(p))),
        pl.BlockSpec((tq, LANES), lambda r, p, i: (i, r * row_width_blocks + k_cb + kvb(p))),
        pl.BlockSpec((tp, LANES), lambda r, p, i: (jnp.maximum(i * ratio - 1, 0), r * row_width_blocks + v_cb + kvb(p))),
        pl.BlockSpec((tq, LANES), lambda r, p, i: (i, r * row_width_blocks + v_cb + kvb(p))),
    ]
    args += [qkv_view] * 5
    if bias is not None:
        in_specs.append(pl.BlockSpec((2, BLOCK, ctx), lambda r, p, i: (p, 0, 0)))
        args.append(bias)
    out_w = nsub * n_qblocks * LANES
    o_spec = pl.BlockSpec((tq, LANES), lambda r, p, i: (i, r * n_qblocks + p))
    out_shape = [jax.ShapeDtypeStruct((nrows, out_w), F32)]
    out_specs = [o_spec]
    if want_lse:
        out_shape.append(jax.ShapeDtypeStruct((nrows, out_w), F32))
        out_specs.append(o_spec)
    kern = functools.partial(_banded_kernel, nprev=nprev, tq=tq, max_dist=max_dist, use_bias=bias is not None,
                             use_sinks=sinks is not None, want_lse=want_lse, group_div=group_div, row_axis=2)
    return pl.pallas_call(
        kern,
        out_shape=tuple(out_shape),
        grid=(nsub, n_qblocks, nrows // tq),
        in_specs=in_specs,
        out_specs=tuple(out_specs),
        scratch_shapes=[pltpu.VMEM((tp + tq, LANES), BF16), pltpu.VMEM((tp + tq, LANES), BF16)],
        compiler_params=_cparams(("arbitrary", "arbitrary", "arbitrary")),
        name=name,
    )(*args)


def _compress_kernel(x_ref, pos_ref, w1_ref, w2_ref, o_ref):
    x = x_ref[...]
    half = CMP_STRIDE * HEAD_DIM
    xa = (x + pos_ref[0:1, :]).astype(BF16)
    xb = (x + pos_ref[1:2, :]).astype(BF16)
    first = _dot(xa, w1_ref[0:half, :])
    second = _dot(xb, w1_ref[half:2 * half, :])
    n = x.shape[0]
    hid = first + pltpu.roll(second, n - 1, axis=0)
    act = jax.nn.gelu(hid)
    o_ref[...] = _dot(act.astype(BF16), w2_ref[...]).astype(BF16)


def _compress(ck, cmp_pos_l, w1_l, w2_l, seq):
    n_chunk = seq // CMP_STRIDE
    feat = CMP_STRIDE * HEAD_DIM
    x = ck.reshape(4, n_chunk, feat)
    pos = cmp_pos_l.reshape(2, 2, feat)
    out = pl.pallas_call(
        _compress_kernel,
        out_shape=jax.ShapeDtypeStruct((4, n_chunk, HEAD_DIM), BF16),
        grid=(4,),
        in_specs=[
            pl.BlockSpec((None, n_chunk, feat), lambda g: (g, 0, 0)),
            pl.BlockSpec((None, 2, feat), lambda g: (g // 2, 0, 0)),
            pl.BlockSpec((None, 2 * feat, CMP_HIDDEN), lambda g: (g // 2, 0, 0)),
            pl.BlockSpec((None, CMP_HIDDEN, HEAD_DIM), lambda g: (g // 2, 0, 0)),
        ],
        out_specs=pl.BlockSpec((None, n_chunk, HEAD_DIM), lambda g: (g, 0, 0)),
        compiler_params=_cparams(("arbitrary",)),
        name="compress_tokens",
    )(x, pos, w1_l, w2_l)
    kc = jnp.concatenate([out[0], out[1]], axis=1)
    vct = out[2:4].reshape(2, n_chunk // BLOCK, BLOCK, HEAD_DIM).transpose(0, 1, 3, 2)
    return kc, vct


def _cmp_attn_kernel(qt_ref, kc_ref, vct_ref, mt_ref, o_ref, mq_ref, e_s, impsel_s, *, n_cmp, n_blk):
    kvh = pl.program_id(0)
    qi = pl.program_id(1)
    row_half = lax.broadcasted_iota(jnp.int32, (LANES, BLOCK), 0) >> 6
    qts = []
    for h in range(4):
        qh = qt_ref[h * HEAD_DIM:(h + 1) * HEAD_DIM, :].astype(F32)
        both = jnp.concatenate([qh, qh], axis=0)
        qts.append(jnp.where(row_half == kvh, both, 0.0).astype(BF16))
    qt_pairs = [jnp.concatenate(qts[0:2], axis=1), jnp.concatenate(qts[2:4], axis=1)]

    krow = lax.broadcasted_iota(jnp.int32, (BLOCK, 2 * BLOCK), 0)
    qcol = lax.broadcasted_iota(jnp.int32, (BLOCK, 2 * BLOCK), 1) & (BLOCK - 1)
    qpos = qi * BLOCK + qcol
    mtl = mt_ref[...]

    def attend(nc):
        m_locs, l_locs, pvs = [], [], []
        for c in range(nc):
            r0 = c * BLOCK
            kcc = kc_ref[r0:r0 + BLOCK, :]
            vcc = vct_ref[c]
            valid = (r0 + krow) * CMP_STRIDE + (CMP_BLOCK - 1) <= qpos
            ml, ll, pl_ = [], [], []
            for pair in range(2):
                cols = slice(pair * 2 * BLOCK, (pair + 1) * 2 * BLOCK)
                s = jnp.where(valid, _dot(kcc, qt_pairs[pair]), NEG)
                m_loc = jnp.max(s, axis=0, keepdims=True)
                e = jnp.where(valid, jnp.exp(s - m_loc), 0.0)
                e_s[r0:r0 + BLOCK, cols] = e
                ml.append(m_loc)
                ll.append(jnp.sum(e, axis=0, keepdims=True))
                pl_.append(_dot(vcc, e.astype(BF16)))
            m_locs.append(ml)
            l_locs.append(ll)
            pvs.append(pl_)
        scales = [[None, None] for _ in range(nc)]
        for pair in range(2):
            m_fin = m_locs[0][pair]
            for c in range(1, nc):
                m_fin = jnp.maximum(m_fin, m_locs[c][pair])
            ws = [jnp.exp(m_locs[c][pair] - m_fin) for c in range(nc)]
            den = ws[0] * l_locs[0][pair]
            acc = ws[0] * pvs[0][pair]
            for c in range(1, nc):
                den = den + ws[c] * l_locs[c][pair]
                acc = acc + ws[c] * pvs[c][pair]
            inv_l = 1.0 / jnp.maximum(den, 1e-30)
            o = acc * inv_l
            o_ref[:, pair * LANES:(pair + 1) * LANES] = jnp.concatenate([o[:, 0:BLOCK], o[:, BLOCK:2 * BLOCK]],
                                                                        axis=0).T
            for c in range(nc):
                scales[c][pair] = ws[c] * inv_l
        impsel_s[...] = jnp.zeros(impsel_s.shape, F32)
        for c in range(nc):
            r0 = c * BLOCK
            pn = e_s[r0:r0 + BLOCK, :] * jnp.concatenate(scales[c], axis=1)
            imp = pn[:, 0:BLOCK] + pn[:, BLOCK:2 * BLOCK] + pn[:, 2 * BLOCK:3 * BLOCK] + pn[:, 3 * BLOCK:4 * BLOCK]
            hi = imp.astype(BF16)
            r1 = imp - hi.astype(F32)
            mid = r1.astype(BF16)
            lo = (r1 - mid.astype(F32)).astype(BF16)
            b0 = c * (BLOCK // 4)
            impsel_s[b0:b0 + 64, :] += _dot(mtl, hi) + _dot(mtl, mid) + _dot(mtl, lo)

    n_ch = n_cmp // BLOCK
    if n_ch >= 4:
        few = qi < (n_ch // 2) * CMP_STRIDE

        @pl.when(few)
        def _():
            attend(n_ch // 2)

        @pl.when(jnp.logical_not(few))
        def _():
            attend(n_ch)
    else:
        attend(n_ch)

    imp_sel = impsel_s[0:n_blk, :]
    blk = lax.broadcasted_iota(jnp.int32, (n_blk, BLOCK), 0)
    cur = (qi * BLOCK + lax.broadcasted_iota(jnp.int32, (n_blk, BLOCK), 1)) >> 6
    forced = (blk == 0) | (blk == cur) | (blk == cur - 1)
    causal = blk <= cur
    score = jnp.where(forced, -jnp.inf, jnp.where(causal, imp_sel, NEG))
    blk_f = blk.astype(F32)
    chosen = jnp.where(forced | (cur < SEL_TOPK), 1.0, 0.0)
    for _ in range(SEL_TOPK - 3):
        mx = jnp.max(score, axis=0, keepdims=True)
        first = jnp.min(jnp.where(score == mx, blk_f, 1e9), axis=0, keepdims=True)
        pick = blk_f == first
        chosen = jnp.where(pick, 1.0, chosen)
        score = jnp.where(pick, -jnp.inf, score)
    mq_ref[...] = jnp.where(causal & (chosen > 0.5), 0.0, NEG).astype(BF16)


def _importance_matrix():
    mt = np.zeros((64, BLOCK), np.float32)
    ratio = SEL_BLOCK // CMP_STRIDE
    for o in range(-(CMP_BLOCK // CMP_STRIDE - 1), ratio):
        w = max(0, min(SEL_BLOCK, o * CMP_STRIDE + CMP_BLOCK) - max(0, o * CMP_STRIDE)) / CMP_BLOCK
        for r in range(BLOCK // ratio + 1):
            k = r * ratio + o
            if 0 <= k < BLOCK:
                mt[r, k] = w
    return jnp.asarray(mt, dtype=BF16)


def _cmp_attn(qt, kc, vct, seq):
    n_cmp = seq // CMP_STRIDE
    n_blk = max(LANES, seq // SEL_BLOCK)
    n_ch = n_cmp // BLOCK
    mt = _importance_matrix()
    kern = functools.partial(_cmp_attn_kernel, n_cmp=n_cmp, n_blk=n_blk)
    return pl.pallas_call(
        kern,
        out_shape=(jax.ShapeDtypeStruct((seq, 512), F32),
                   jax.ShapeDtypeStruct((2, seq // BLOCK, n_blk, BLOCK), BF16)),
        grid=(2, seq // BLOCK),
        in_specs=[
            pl.BlockSpec((4 * HEAD_DIM, BLOCK), lambda h, i: (h, i)),
            pl.BlockSpec((n_cmp, LANES), lambda h, i: (0, 0)),
            pl.BlockSpec((None, n_ch, HEAD_DIM, BLOCK), lambda h, i: (h, 0, 0, 0)),
            pl.BlockSpec((64, BLOCK), lambda h, i: (0, 0)),
        ],
        out_specs=(pl.BlockSpec((BLOCK, 256), lambda h, i: (i, h)),
                   pl.BlockSpec((None, None, n_blk, BLOCK), lambda h, i: (h, i, 0, 0))),
        scratch_shapes=[pltpu.VMEM((n_cmp, 4 * BLOCK), F32),
                        pltpu.VMEM((n_blk + 64, BLOCK), F32)],
        compiler_params=_cparams(("arbitrary", "arbitrary")),
        name="cmp_attn_topk",
    )(qt, kc, vct, mt)


def _sel_kernel(qi_ref, kt_ref, qt_ref, mq_ref, k_ref, vt_ref, e_ref, tab_ref, b31_ref, o_ref,
                qaug, m_s, acc_s, *, n_e):
    kvh = pl.program_id(0)
    step = pl.program_id(1)
    qi = qi_ref[step]
    kt = kt_ref[step]
    q_per_sub = SEL_SUB // SEL_TQ
    win_steps = LANES * SEL_BLOCK // SEL_TK
    ncol = 4 * SEL_TQ

    @pl.when(kt == 0)
    def _():
        m_s[...] = jnp.full(m_s.shape, -3e38, F32)
        acc_s[...] = jnp.zeros(acc_s.shape, F32)

    @pl.when(kt % win_steps == 0)
    def _():
        row_half = lax.broadcasted_iota(jnp.int32, (LANES, SEL_TQ), 0) >> 6
        mq = mq_ref[...]
        for h in range(4):
            qh = qt_ref[h * HEAD_DIM:(h + 1) * HEAD_DIM, :].astype(F32)
            both = jnp.concatenate([qh, qh], axis=0)
            qaug[0:LANES, h * SEL_TQ:(h + 1) * SEL_TQ] = jnp.where(row_half == kvh, both, 0.0).astype(BF16)
            qaug[LANES:2 * LANES, h * SEL_TQ:(h + 1) * SEL_TQ] = mq

    sub_near0 = jnp.maximum(0, (qi - SEL_SLAB_HI + q_per_sub - 1) // q_per_sub)

    def run(with_bias):
        for pair in range(2):
            cols = slice(pair * 2 * SEL_TQ, (pair + 1) * 2 * SEL_TQ)
            qa = qaug[:, cols]
            if with_bias:
                b31 = jnp.concatenate([jnp.full((1, SEL_TQ), b31_ref[kvh * 4 + pair * 2 + hh], F32)
                                       for hh in range(2)], axis=1)
            m_locs, pvs = [], []
            for j in range(SEL_NSUB):
                sub = kt * SEL_NSUB + j
                is_far = sub < sub_near0
                for c in range(SEL_SUB // SEL_CHUNK):
                    r0 = j * SEL_SUB + c * SEL_CHUNK
                    kaug = jnp.concatenate([k_ref[r0:r0 + SEL_CHUNK, :],
                                            e_ref[sub % n_e, c * SEL_CHUNK:(c + 1) * SEL_CHUNK, :]], axis=1)
                    s = _dot(kaug, qa)
                    if with_bias:
                        blocks = []
                        for hh in range(2):
                            col = []
                            for tt in range(SEL_CHUNK // BLOCK):
                                t = c * (SEL_CHUNK // BLOCK) + tt
                                u = jnp.maximum(qi - q_per_sub * sub - t, SEL_SLAB_LO)
                                uidx = jnp.where(is_far, SEL_NSLAB - 1, u - SEL_SLAB_LO)
                                col.append(s[tt * BLOCK:(tt + 1) * BLOCK, hh * SEL_TQ:(hh + 1) * SEL_TQ]
                                           + tab_ref[pair * 2 + hh, uidx])
                            blocks.append(jnp.concatenate(col, axis=0))
                        s = jnp.concatenate(blocks, axis=1)
                    m_loc = jnp.max(s, axis=0, keepdims=True)
                    pr = jnp.exp(s - m_loc).astype(BF16)
                    pvs.append(_dot(vt_ref[0:SEL_VROWS, r0:r0 + SEL_CHUNK], pr))
                    if with_bias:
                        m_loc = m_loc + jnp.where(is_far, b31, 0.0)
                    m_locs.append(m_loc)
            m_run = m_s[:, cols]
            if with_bias:
                m_run = m_run + jnp.where(sub_near0 >= kt * SEL_NSUB, b31, 0.0)
            m_new = m_run
            for m_loc in m_locs:
                m_new = jnp.maximum(m_new, m_loc)
            acc = jnp.exp(m_run - m_new) * acc_s[:, cols]
            for m_loc, pv in zip(m_locs, pvs):
                acc = acc + jnp.exp(m_loc - m_new) * pv
            acc_s[:, cols] = acc
            m_s[:, cols] = m_new

    all_far = kt * SEL_NSUB + SEL_NSUB - 1 < sub_near0

    @pl.when(all_far)
    def _():
        run(False)

    @pl.when(jnp.logical_not(all_far))
    def _():
        run(True)

    @pl.when(kt == qi // (SEL_TK // SEL_TQ))
    def _():
        acc = acc_s[...]
        o = acc[0:HEAD_DIM, :] / acc[HEAD_DIM:HEAD_DIM + 1, :]
        for jp in range(2):
            blk = jnp.concatenate([o[:, (2 * jp) * SEL_TQ:(2 * jp + 1) * SEL_TQ],
                                   o[:, (2 * jp + 1) * SEL_TQ:(2 * jp + 2) * SEL_TQ]], axis=0)
            o_ref[:, jp * LANES:(jp + 1) * LANES] = blk.T


def _sel_attn(qkv, qt, vt, mq, bias_s, rel_bias, seq):
    n_q = seq // SEL_TQ
    per_step = SEL_TK // SEL_TQ
    qi_l, kt_l = [], []
    for qi in range(n_q):
        for kt in range(qi // per_step + 1):
            qi_l.append(qi)
            kt_l.append(kt)
    qi_arr = jnp.asarray(np.array(qi_l, np.int32))
    kt_arr = jnp.asarray(np.array(kt_l, np.int32))
    win_steps = LANES * SEL_BLOCK // SEL_TK
    n_e = min(LANES * SEL_BLOCK // SEL_SUB, seq // SEL_SUB)
    e_np = np.zeros((n_e, SEL_SUB, LANES), np.float32)
    for t in range(n_e):
        for k in range(SEL_SUB):
            e_np[t, k, (t * (SEL_SUB // SEL_BLOCK) + k // SEL_BLOCK) % LANES] = 1.0
    e_all = jnp.asarray(e_np, dtype=BF16)
    b31 = rel_bias[REL_BUCKETS - 1, 4:12]
    grid_spec = pltpu.PrefetchScalarGridSpec(
        num_scalar_prefetch=2,
        grid=(2, len(qi_l)),
        in_specs=[
            pl.BlockSpec((4 * HEAD_DIM, SEL_TQ), lambda h, s, qi, kt: (h, qi[s])),
            pl.BlockSpec((None, None, LANES, SEL_TQ), lambda h, s, qi, kt: (h, qi[s], kt[s] // win_steps, 0)),
            pl.BlockSpec((SEL_TK, LANES), lambda h, s, qi, kt: (kt[s], _qkv_col(COL_CKS) // LANES)),
            pl.BlockSpec((None, LANES, SEL_TK), lambda h, s, qi, kt: (h, 0, kt[s])),
            pl.BlockSpec((n_e, SEL_SUB, LANES), lambda h, s, qi, kt: (0, 0, 0)),
            pl.BlockSpec((4, SEL_NSLAB, BLOCK, BLOCK), lambda h, s, qi, kt: (h, 0, 0, 0)),
            pl.BlockSpec(memory_space=pltpu.SMEM),
        ],
        out_specs=pl.BlockSpec((SEL_TQ, 256), lambda h, s, qi, kt: (qi[s], h)),
        scratch_shapes=[pltpu.VMEM((2 * LANES, 4 * SEL_TQ), BF16),
                        pltpu.VMEM((1, 4 * SEL_TQ), F32),
                        pltpu.VMEM((SEL_VROWS, 4 * SEL_TQ), F32)],
    )
    return pl.pallas_call(
        functools.partial(_sel_kernel, n_e=n_e),
        out_shape=jax.ShapeDtypeStruct((seq, 512), F32),
        grid_spec=grid_spec,
        compiler_params=_cparams(("arbitrary", "arbitrary")),
        name="selected_attn",
    )(qi_arr, kt_arr, qt, mq, qkv, vt, e_all, bias_s, b31)


def _layer_norm(z, g, b):
    mu = jnp.mean(z, axis=-1, keepdims=True)
    zc = z - mu
    var = jnp.mean(jnp.square(zc), axis=-1, keepdims=True)
    return zc * lax.rsqrt(var + LN_EPS) * g + b


def _outproj_kernel(x_ref, oa0, oa1, oa2, la0, la1, la2, ob_ref, ocmp_ref, oslc_ref, owin_ref, g_ref,
                    gx_ref, w_ref, ga_ref, lng_ref, lnb_ref, o_ref):
    l0, l1, l2 = la0[...], la1[...], la2[...]
    mx = jnp.maximum(jnp.maximum(l0, l1), l2)
    w0, w1, w2 = jnp.exp(l0 - mx), jnp.exp(l1 - mx), jnp.exp(l2 - mx)
    oa = (w0 * oa0[...] + w1 * oa1[...] + w2 * oa2[...]) / (w0 + w1 + w2)
    gt = jax.nn.sigmoid(g_ref[...])
    hi = gt.astype(BF16)
    lo = (gt - hi.astype(F32)).astype(BF16)
    gx = gx_ref[...]
    gates = _dot(hi, gx) + _dot(lo, gx)
    oc = gates[:, 0:512] * ocmp_ref[...] + gates[:, 512:1024] * oslc_ref[...] + gates[:, 1024:1536] * owin_ref[...]
    mixed = jnp.concatenate([oa, ob_ref[...], oc], axis=1).astype(BF16)
    y = _dot(mixed, w_ref[...])
    z = DN_ALPHA * x_ref[...] + (1 + ga_ref[...]) * y
    o_ref[...] = _layer_norm(z, lng_ref[...], lnb_ref[...])


def _gate_expand():
    gx = np.zeros((LANES, 3 * 512), np.float32)
    for h in range(8):
        for b in range(3):
            gx[h * 3 + b, b * 512 + h * HEAD_DIM:b * 512 + (h + 1) * HEAD_DIM] = 1.0
    return jnp.asarray(gx, dtype=BF16)


def _outproj(x, oa, la, ob, ocmp, oslc, owin, gates, w_out_l, mod_l, lng, lnb, seq):
    tm = 256
    row = lambda w: pl.BlockSpec((tm, w), lambda i: (i, 0))
    vec = lambda k: pl.BlockSpec((1, D_MODEL), lambda i, k=k: (0, k))
    return pl.pallas_call(
        _outproj_kernel,
        out_shape=jax.ShapeDtypeStruct((seq, D_MODEL), F32),
        grid=(seq // tm,),
        in_specs=[row(D_MODEL)] + [row(256)] * 7 + [row(512)] * 3 + [row(LANES)] + [
            pl.BlockSpec((LANES, 3 * 512), lambda i: (0, 0)),
            pl.BlockSpec((D_MODEL, D_MODEL), lambda i: (0, 0)),
            vec(2), vec(0), vec(0)],
        out_specs=row(D_MODEL),
        compiler_params=_cparams(("arbitrary",)),
        name="out_proj_ln",
    )(x, oa[0], oa[1], oa[2], la[0], la[1], la[2], ob, ocmp, oslc, owin, gates, _gate_expand(), w_out_l,
      mod_l, lng, lnb)


def _mlp_kernel(x_ref, sc_ref, sh_ref, g_ref, w1_ref, w2_ref, lng_ref, lnb_ref, o_ref, h_s, acc_s):
    j = pl.program_id(1)

    @pl.when(j == 0)
    def _():
        h_s[...] = (x_ref[...] * (1 + sc_ref[...]) + sh_ref[...]).astype(BF16)
        acc_s[...] = jnp.zeros(acc_s.shape, F32)

    f = jnp.maximum(_dot(h_s[...], w1_ref[...]), 0.0)
    acc_s[...] += _dot(jnp.square(f).astype(BF16), w2_ref[...])

    @pl.when(j == pl.num_programs(1) - 1)
    def _():
        z = DN_ALPHA * x_ref[...] + (1 + g_ref[...]) * acc_s[...]
        o_ref[...] = _layer_norm(z, lng_ref[...], lnb_ref[...])


def _mlp(x, mod_l, w1_l, w2_l, lng, lnb, seq):
    tm = 1024
    tf = 1024
    vec = lambda k: pl.BlockSpec((1, D_MODEL), lambda i, j, k=k: (0, k))
    return pl.pallas_call(
        _mlp_kernel,
        out_shape=jax.ShapeDtypeStruct((seq, D_MODEL), F32),
        grid=(seq // tm, D_FF // tf),
        in_specs=[
            pl.BlockSpec((tm, D_MODEL), lambda i, j: (i, 0)),
            vec(4), vec(3), vec(5),
            pl.BlockSpec((D_MODEL, tf), lambda i, j: (0, j)),
            pl.BlockSpec((tf, D_MODEL), lambda i, j: (j, 0)),
            vec(0), vec(0),
        ],
        out_specs=pl.BlockSpec((tm, D_MODEL), lambda i, j: (i, 0)),
        scratch_shapes=[pltpu.VMEM((tm, D_MODEL), BF16), pltpu.VMEM((tm, D_MODEL), F32)],
        compiler_params=_cparams(("arbitrary", "arbitrary")),
        name="mlp_ln",
    )(x, mod_l, mod_l, mod_l, w1_l, w2_l, lng, lnb)


def kernel(x, c, positions, w_in, w_out, rel_bias, sinks, cmp_pos, cmp_w1, cmp_w2, mlp_w1, mlp_w2,
           ada_w, ada_b, ln_g, ln_b):
    bsz, seq, d = x.shape
    assert bsz == 1 and d == D_MODEL and seq % (2 * A_CONFIGS[-1][0]) == 0
    x = x.reshape(seq, D_MODEL)

    mod = _modulation(c, ada_w, ada_b)
    cos_t, sin_t = _rope_tables(positions, seq)
    bias_a, bias_w, bias_s = _bias_tables(rel_bias)

    w_in_b = jnp.pad(w_in, ((0, 0), (0, 0), (0, N_IN_PAD - N_IN))).astype(BF16)
    w_out_b = w_out.astype(BF16)
    w1_b = mlp_w1.astype(BF16)
    w2_b = mlp_w2.astype(BF16)
    cw1_b = cmp_w1.astype(BF16)
    cw2_b = cmp_w2.astype(BF16)

    for l in range(DEPTH):
        mod_l = mod[l]
        qkv, ck, gates, qt, vt, a_qkv = _inproj(x, mod_l, w_in_b[l], cos_t, sin_t, seq)

        oa, la = [], []
        for ci, (window, dil) in enumerate(A_CONFIGS):
            nrows = seq // dil
            view = a_qkv.reshape(nrows, dil * N_A)
            o_c, l_c = _banded_call(
                view, nrows, dil, N_A // LANES, 0, 2, 4, 2,
                nprev=1, tq=min(BAND_TQ, nrows), max_dist=window // dil, bias=bias_a[ci], want_lse=True,
                group_div=None, kv_per_qblock=True, name=f"dilated_attn_{dil}")
            oa.append(o_c.reshape(seq, 256))
            la.append(l_c.reshape(seq, 256))

        (ob,) = _banded_call(
            qkv, seq, 1, N_QKV // LANES, _qkv_col(COL_BQ) // LANES, _qkv_col(COL_BK) // LANES,
            _qkv_col(COL_BV) // LANES, 2,
            nprev=1, tq=BAND_TQ, max_dist=B_WINDOW - 1, sinks=sinks[l], group_div=1,
            kv_per_qblock=False, name="swa_attn")

        kc, vct = _compress(ck, cmp_pos[l], cw1_b[l], cw2_b[l], seq)
        ocmp, mq = _cmp_attn(qt, kc, vct, seq)
        oslc = _sel_attn(qkv, qt, vt, mq, bias_s, rel_bias, seq)
        (owin,) = _banded_call(
            qkv, seq, 1, N_QKV // LANES, _qkv_col(COL_CQ) // LANES, _qkv_col(COL_CKW) // LANES,
            _qkv_col(COL_CVW) // LANES, 4,
            nprev=C_WINDOW // BLOCK, tq=BAND_TQ, max_dist=C_WINDOW - 1, bias=bias_w, group_div=2,
            kv_per_qblock=False, name="window_attn")

        x = _outproj(x, oa, la, ob, ocmp, oslc, owin, gates, w_out_b[l], mod_l,
                     ln_g[l, 0:1], ln_b[l, 0:1], seq)
        x = _mlp(x, mod_l, w1_b[l], w2_b[l], ln_g[l, 1:2], ln_b[l, 1:2], seq)

    return x.reshape(bsz, seq, D_MODEL)
```

```python
import functools
import math

import numpy as np
import jax
import jax.numpy as jnp
from jax import lax
from jax.experimental import pallas as pl
from jax.experimental.pallas import tpu as pltpu

F32 = jnp.float32
BF16 = jnp.bfloat16

D_MODEL = 1024
DEPTH = 4
HEAD_DIM = 64
LANES = 128
BLOCK = 128
A_CONFIGS = ((128, 1), (512, 4), (2048, 16))
B_WINDOW = 128
ROPE_THETA = 150000.0
CMP_BLOCK = 32
CMP_STRIDE = 16
CMP_HIDDEN = 256
SEL_BLOCK = 64
SEL_TOPK = 16
C_WINDOW = 512
REL_BUCKETS = 32
REL_MAX_DIST = 2048
D_FF = 4 * D_MODEL
DN_ALPHA = (2 * DEPTH) ** 0.25
LN_EPS = 1e-5
NEG = -1e30
FORCE = 1e4
SCALE = HEAD_DIM ** -0.5

COL_AQ, COL_AK, COL_AV = 0, 256, 512
COL_BQ, COL_BK, COL_BV = 768, 1024, 1152
COL_CQ = 1280
COL_CKC = 1792
COL_CKS, COL_CVS = 2048, 2176
COL_CKW, COL_CVW = 2304, 2432
COL_CG = 2560
N_A = 768
N_QKV = 1536


def _qkv_col(col):
    assert COL_BQ <= col < COL_CKC or COL_CKS <= col < COL_CG
    return col - N_A if col < COL_CKC else col - N_A - 256
N_IN = 2584
N_IN_PAD = 2688

BAND_TQ = 1024
SEL_TQ = 128
SEL_SUB = 512
SEL_CHUNK = 128
SEL_VROWS = 80
SEL_NSUB = 4
SEL_TK = SEL_SUB * SEL_NSUB
SEL_SLAB_LO = -1
SEL_SLAB_HI = 15
SEL_NSLAB = SEL_SLAB_HI - SEL_SLAB_LO + 2
VMEM_LIMIT = 56 * 1024 * 1024


def _cparams(sem, flags=None):
    return pltpu.CompilerParams(dimension_semantics=sem, vmem_limit_bytes=VMEM_LIMIT, flags=flags)


def _dot_nt(a, b):
    return lax.dot_general(a, b, (((1,), (1,)), ((), ())), preferred_element_type=F32)


def _dot(a, b):
    return jnp.dot(a, b, preferred_element_type=F32)


def _mod_kernel(c_ref, w_ref, b_ref, o_ref):
    c = c_ref[...]
    act = c * jax.nn.sigmoid(c)
    lhs = jnp.broadcast_to(act, (8, D_MODEL)).astype(BF16)
    y = _dot(lhs, w_ref[...].astype(BF16))
    o_ref[...] = y[0:1, :] + b_ref[...]


def _modulation(c, ada_w, ada_b):
    tn = 1536
    return pl.pallas_call(
        _mod_kernel,
        out_shape=jax.ShapeDtypeStruct((DEPTH, 1, 6 * D_MODEL), F32),
        grid=(DEPTH, 6 * D_MODEL // tn),
        in_specs=[
            pl.BlockSpec((1, D_MODEL), lambda l, j: (0, 0)),
            pl.BlockSpec((None, D_MODEL, tn), lambda l, j: (l, 0, j)),
            pl.BlockSpec((None, 1, tn), lambda l, j: (l, 0, j)),
        ],
        out_specs=pl.BlockSpec((None, 1, tn), lambda l, j: (l, 0, j)),
        compiler_params=_cparams(("arbitrary", "arbitrary")),
        name="adaln_mod",
    )(c, ada_w, ada_b.reshape(DEPTH, 1, 6 * D_MODEL))


def _rope_table_kernel(pos_ref, freq_ref, cos_ref, sin_ref):
    ang = pos_ref[...].astype(F32) * freq_ref[...]
    cos_ref[...] = jnp.cos(ang)
    sin_ref[...] = jnp.sin(ang)


def _rope_tables(positions, seq):
    half = HEAD_DIM // 2
    freq = ROPE_THETA ** (-jnp.arange(half, dtype=F32) / half)
    freq = jnp.tile(freq, LANES // half).reshape(1, LANES)
    tm = 1024
    return pl.pallas_call(
        _rope_table_kernel,
        out_shape=(jax.ShapeDtypeStruct((seq, LANES), F32),) * 2,
        grid=(seq // tm,),
        in_specs=[pl.BlockSpec((tm, 1), lambda i: (i, 0)), pl.BlockSpec((1, LANES), lambda i: (0, 0))],
        out_specs=(pl.BlockSpec((tm, LANES), lambda i: (i, 0)),) * 2,
        compiler_params=_cparams(("arbitrary",)),
        name="rope_tables",
    )(positions.reshape(seq, 1), freq)


def _t5_bias(rb_ref, dist, col, max_dist):
    exact = REL_BUCKETS // 2
    d = jnp.maximum(dist, 0)
    df = jnp.maximum(d, 1).astype(F32)
    large = exact + (jnp.log(df / exact) / math.log(REL_MAX_DIST / exact)
                     * (REL_BUCKETS - exact)).astype(jnp.int32)
    large = jnp.minimum(large, REL_BUCKETS - 1)
    bucket = jnp.where(d < exact, d, large)
    out = jnp.zeros(dist.shape, F32)
    for b in range(REL_BUCKETS):
        out = jnp.where(bucket == b, rb_ref[b * 12 + col], out)
    return out, bucket


def _bias_a_kernel(rb_ref, o_ref):
    cfg = pl.program_id(0)
    head = pl.program_id(1)
    dil = jnp.where(cfg == 0, 1, jnp.where(cfg == 1, 4, 16))
    r = lax.broadcasted_iota(jnp.int32, (BLOCK, 2 * BLOCK), 0)
    c = lax.broadcasted_iota(jnp.int32, (BLOCK, 2 * BLOCK), 1)
    dist = r + BLOCK - c
    bias, _ = _t5_bias(rb_ref, dist * dil, head, None)
    o_ref[...] = jnp.where((dist >= 0) & (dist <= BLOCK), bias, NEG)


def _bias_w_kernel(rb_ref, o_ref):
    head = pl.program_id(0)
    nprev = C_WINDOW // BLOCK
    r = lax.broadcasted_iota(jnp.int32, (BLOCK, (nprev + 1) * BLOCK), 0)
    c = lax.broadcasted_iota(jnp.int32, (BLOCK, (nprev + 1) * BLOCK), 1)
    dist = r + nprev * BLOCK - c
    bias, _ = _t5_bias(rb_ref, dist, 4 + head, None)
    o_ref[...] = jnp.where((dist >= 0) & (dist <= C_WINDOW - 1), bias, NEG)


def _bias_s_kernel(rb_ref, o_ref):
    head = pl.program_id(0)
    slab = pl.program_id(1)
    r = lax.broadcasted_iota(jnp.int32, (BLOCK, BLOCK), 1)
    c = lax.broadcasted_iota(jnp.int32, (BLOCK, BLOCK), 0)
    dist = (slab + SEL_SLAB_LO) * BLOCK + r - c
    bias, _ = _t5_bias(rb_ref, dist, 4 + head, None)
    val = jnp.where(dist >= 0, bias, NEG)
    o_ref[...] = jnp.where(slab == SEL_NSLAB - 1, 0.0, val)


def _bias_tables(rel_bias):
    rb = rel_bias.reshape(-1)
    smem = pl.BlockSpec(memory_space=pltpu.SMEM)
    bias_a = pl.pallas_call(
        _bias_a_kernel,
        out_shape=jax.ShapeDtypeStruct((3, 4, BLOCK, 2 * BLOCK), F32),
        grid=(3, 4),
        in_specs=[smem],
        out_specs=pl.BlockSpec((None, None, BLOCK, 2 * BLOCK), lambda a, h: (a, h, 0, 0)),
        compiler_params=_cparams(("arbitrary", "arbitrary")),
        name="bias_table_dilated",
    )(rb)
    wctx = C_WINDOW + BLOCK
    bias_w = pl.pallas_call(
        _bias_w_kernel,
        out_shape=jax.ShapeDtypeStruct((8, BLOCK, wctx), F32),
        grid=(8,),
        in_specs=[smem],
        out_specs=pl.BlockSpec((None, BLOCK, wctx), lambda h: (h, 0, 0)),
        compiler_params=_cparams(("arbitrary",)),
        name="bias_table_window",
    )(rb)
    bias_s = pl.pallas_call(
        _bias_s_kernel,
        out_shape=jax.ShapeDtypeStruct((8, SEL_NSLAB, BLOCK, BLOCK), F32),
        grid=(8, SEL_NSLAB),
        in_specs=[smem],
        out_specs=pl.BlockSpec((None, None, BLOCK, BLOCK), lambda h, u: (h, u, 0, 0)),
        compiler_params=_cparams(("arbitrary", "arbitrary")),
        name="bias_table_selected",
    )(rb)
    return bias_a, bias_w, bias_s


def _rope_apply(t, cos, sin, first):
    up = pltpu.roll(t, LANES - HEAD_DIM // 2, axis=1)
    dn = pltpu.roll(t, HEAD_DIM // 2, axis=1)
    return t * cos + jnp.where(first, -up, dn) * sin


def _inproj_kernel(x_ref, sc_ref, sh_ref, w_ref, cos_ref, sin_ref, qkv_ref, ck_ref, g_ref, qt_ref, vt_ref,
                   a_ref):
    h = (x_ref[...] * (1 + sc_ref[...]) + sh_ref[...]).astype(BF16)
    y = _dot(h, w_ref[...])
    cos = cos_ref[...]
    sin = sin_ref[...]
    lane = lax.broadcasted_iota(jnp.int32, cos.shape, 1)
    first = (lane & (HEAD_DIM - 1)) < HEAD_DIM // 2

    def put(c0, width, val):
        q0 = _qkv_col(c0)
        qkv_ref[:, q0:q0 + width] = val.astype(BF16)

    a_ref[:, 0:256] = (y[:, COL_AQ:COL_AQ + 256] * SCALE).astype(BF16)
    a_ref[:, 256:768] = y[:, COL_AK:COL_AK + 512].astype(BF16)
    for j in range(2):
        c0 = COL_BQ + j * LANES
        put(c0, LANES, _rope_apply(y[:, c0:c0 + LANES], cos, sin, first) * SCALE)
    put(COL_BK, LANES, _rope_apply(y[:, COL_BK:COL_BK + LANES], cos, sin, first))
    put(COL_BV, LANES, y[:, COL_BV:COL_BV + LANES])
    put(COL_CQ, 512, y[:, COL_CQ:COL_CQ + 512] * SCALE)
    put(COL_CKS, 512, y[:, COL_CKS:COL_CKS + 512])
    for g in range(4):
        c0 = COL_CKC + g * HEAD_DIM
        ck_ref[g] = y[:, c0:c0 + HEAD_DIM]
    g_ref[...] = y[:, COL_CG:COL_CG + LANES]
    qt_ref[...] = (y[:, COL_CQ:COL_CQ + 512] * SCALE).T.astype(BF16)
    vt = y[:, COL_CVS:COL_CVS + LANES].T
    ones_row = (lax.broadcasted_iota(jnp.int32, (HEAD_DIM, vt.shape[1]), 0) == 0).astype(F32)
    for kvh in range(2):
        vt_ref[kvh] = jnp.concatenate([vt[kvh * HEAD_DIM:(kvh + 1) * HEAD_DIM, :], ones_row], axis=0).astype(BF16)


def _inproj(x, mod_l, w_in_l, cos_t, sin_t, seq):
    tm = 512
    vec = lambda k: pl.BlockSpec((1, D_MODEL), lambda i, k=k: (0, k))
    return pl.pallas_call(
        _inproj_kernel,
        out_shape=(jax.ShapeDtypeStruct((seq, N_QKV), BF16),
                   jax.ShapeDtypeStruct((4, seq, HEAD_DIM), F32),
                   jax.ShapeDtypeStruct((seq, LANES), F32),
                   jax.ShapeDtypeStruct((512, seq), BF16),
                   jax.ShapeDtypeStruct((2, LANES, seq), BF16),
                   jax.ShapeDtypeStruct((seq, N_A), BF16)),
        grid=(seq // tm,),
        in_specs=[
            pl.BlockSpec((tm, D_MODEL), lambda i: (i, 0)),
            vec(1), vec(0),
            pl.BlockSpec((D_MODEL, N_IN_PAD), lambda i: (0, 0)),
            pl.BlockSpec((tm, LANES), lambda i: (i, 0)),
            pl.BlockSpec((tm, LANES), lambda i: (i, 0)),
        ],
        out_specs=(pl.BlockSpec((tm, N_QKV), lambda i: (i, 0)),
                   pl.BlockSpec((4, tm, HEAD_DIM), lambda i: (0, i, 0)),
                   pl.BlockSpec((tm, LANES), lambda i: (i, 0)),
                   pl.BlockSpec((512, tm), lambda i: (0, i)),
                   pl.BlockSpec((2, LANES, tm), lambda i: (0, 0, i)),
                   pl.BlockSpec((tm, N_A), lambda i: (i, 0))),
        compiler_params=_cparams(("arbitrary",)),
        name="in_proj",
    )(x, mod_l, mod_l, w_in_l, cos_t, sin_t)


def _lane_half():
    return lax.broadcasted_iota(jnp.int32, (BLOCK, LANES), 1) >> 6


def _align_q(q, half, x, ysel):
    qx = jnp.where(half == x, q, 0.0)
    if ysel is None:
        return qx.astype(BF16)
    qd = qx + pltpu.roll(qx, HEAD_DIM, axis=1)
    return jnp.where(ysel, qd, 0.0).astype(BF16)


def _spread_o(o, ysel):
    if ysel is None:
        return o
    ob = jnp.where(ysel, o, 0.0)
    return ob + pltpu.roll(ob, HEAD_DIM, axis=1)


def _banded_kernel(*refs, nprev, tq, max_dist, use_bias, use_sinks, want_lse, group_div, row_axis):
    refs = list(refs)
    sink_ref = refs.pop(0) if use_sinks else None
    q_ref, kp_ref, kc_ref, vp_ref, vc_ref = refs[:5]
    refs = refs[5:]
    bias_ref = refs.pop(0) if use_bias else None
    o_ref = refs.pop(0)
    lse_ref = refs.pop(0) if want_lse else None
    kctx, vctx = refs

    tp = nprev * BLOCK
    ctx = (nprev + 1) * BLOCK
    i = pl.program_id(row_axis)
    p = pl.program_id(row_axis - 1)
    kctx[0:tp, :] = kp_ref[...]
    kctx[tp:tp + tq, :] = kc_ref[...]
    vctx[0:tp, :] = vp_ref[...]
    vctx[tp:tp + tq, :] = vc_ref[...]

    half = _lane_half()
    upper = half == 1
    ysel = None if group_div is None else half == p // group_div
    chunks = [(c0, min(c0 + 2 * BLOCK, ctx)) for c0 in range(0, ctx, 2 * BLOCK)]
    rrs, ccs, bands = [], [], []
    for c0, c1 in chunks:
        rr = lax.broadcasted_iota(jnp.int32, (BLOCK, c1 - c0), 0)
        cc = lax.broadcasted_iota(jnp.int32, (BLOCK, c1 - c0), 1) + c0
        ccs.append(cc)
        if not use_bias:
            dist = rr + tp - cc
            bands.append((dist >= 0) & (dist <= max_dist))

    for sub in range(tq // BLOCK):
        rows = slice(sub * BLOCK, (sub + 1) * BLOCK)
        q = q_ref[rows, :].astype(F32)
        low = tp - (i * tq + sub * BLOCK)
        outs, lses = [], []
        for x in range(2):
            qa = _align_q(q, half, x, ysel)
            ms, dens, pvs = [], [], []
            for ci, (c0, c1) in enumerate(chunks):
                k = kctx[sub * BLOCK + c0:sub * BLOCK + c1, :]
                v = vctx[sub * BLOCK + c0:sub * BLOCK + c1, :]
                s = _dot_nt(qa, k)
                if use_bias:
                    s = s + bias_ref[x, :, c0:c1]
                else:
                    s = jnp.where(bands[ci], s, NEG)
                s = jnp.where(ccs[ci] >= low, s, NEG)
                m_loc = jnp.max(s, axis=1, keepdims=True)
                e = jnp.exp(s - m_loc)
                ms.append(m_loc)
                dens.append(jnp.sum(e, axis=1, keepdims=True))
                pvs.append(_dot(e.astype(BF16), v))
            m = ms[0]
            for m_loc in ms[1:]:
                m = jnp.maximum(m, m_loc)
            if use_sinks:
                sk = sink_ref[2 * p + x]
                m = jnp.maximum(m, sk)
            ws = [jnp.exp(m_loc - m) for m_loc in ms]
            den = ws[0] * dens[0]
            acc = ws[0] * pvs[0]
            for w, d, pv in zip(ws[1:], dens[1:], pvs[1:]):
                den = den + w * d
                acc = acc + w * pv
            if use_sinks:
                den = den + jnp.exp(sk - m)
            o = acc / den
            outs.append(_spread_o(o, ysel))
            lses.append(m + jnp.log(den))
        o_ref[rows, :] = jnp.where(upper, outs[1], outs[0])
        if want_lse:
            lse_ref[rows, :] = jnp.where(upper, jnp.broadcast_to(lses[1], (BLOCK, LANES)),
                                         jnp.broadcast_to(lses[0], (BLOCK, LANES)))


def _banded_call(qkv_view, nrows, nsub, row_width_blocks, q_cb, k_cb, v_cb, n_qblocks, *, nprev, tq,
                 max_dist, bias=None, sinks=None, want_lse=False, group_div=None, kv_per_qblock, name):
    tp = nprev * BLOCK
    ratio = tq // tp
    ctx = (nprev + 1) * BLOCK

    def kvb(p):
        return p if kv_per_qblock else 0

    in_specs = []
    args = []
    if sinks is not None:
        in_specs.append(pl.BlockSpec(memory_space=pltpu.SMEM))
        args.append(sinks)
    in_specs += [
        pl.BlockSpec((tq, LANES), lambda r, p, i: (i, r * row_width_blocks + q_cb + p)),
        pl.BlockSpec((tp, LANES), lambda r, p, i: (jnp.maximum(i * ratio - 1, 0), r * row_width_blocks + k_cb + kvb(p))),
        pl.BlockSpec((tq, LANES), lambda r, p, i: (i, r * row_width_blocks + k_cb + kvb(p))),
        pl.BlockSpec((tp, LANES), lambda r, p, i: (jnp.maximum(i * ratio - 1, 0), r * row_width_blocks + v_cb + kvb(p))),
        pl.BlockSpec((tq, LANES), lambda r, p, i: (i, r * row_width_blocks + v_cb + kvb(p))),
    ]
    args += [qkv_view] * 5
    if bias is not None:
        in_specs.append(pl.BlockSpec((2, BLOCK, ctx), lambda r, p, i: (p, 0, 0)))
        args.append(bias)
    out_w = nsub * n_qblocks * LANES
    o_spec = pl.BlockSpec((tq, LANES), lambda r, p, i: (i, r * n_qblocks + p))
    out_shape = [jax.ShapeDtypeStruct((nrows, out_w), F32)]
    out_specs = [o_spec]
    if want_lse:
        out_shape.append(jax.ShapeDtypeStruct((nrows, out_w), F32))
        out_specs.append(o_spec)
    kern = functools.partial(_banded_kernel, nprev=nprev, tq=tq, max_dist=max_dist, use_bias=bias is not None,
                             use_sinks=sinks is not None, want_lse=want_lse, group_div=group_div, row_axis=2)
    return pl.pallas_call(
        kern,
        out_shape=tuple(out_shape),
        grid=(nsub, n_qblocks, nrows // tq),
        in_specs=in_specs,
        out_specs=tuple(out_specs),
        scratch_shapes=[pltpu.VMEM((tp + tq, LANES), BF16), pltpu.VMEM((tp + tq, LANES), BF16)],
        compiler_params=_cparams(("arbitrary", "arbitrary", "arbitrary")),
        name=name,
    )(*args)


def _compress_kernel(x_ref, pos_ref, w1_ref, w2_ref, o_ref):
    x = x_ref[...]
    half = CMP_STRIDE * HEAD_DIM
    xa = (x + pos_ref[0:1, :]).astype(BF16)
    xb = (x + pos_ref[1:2, :]).astype(BF16)
    first = _dot(xa, w1_ref[0:half, :])
    second = _dot(xb, w1_ref[half:2 * half, :])
    n = x.shape[0]
    hid = first + pltpu.roll(second, n - 1, axis=0)
    act = jax.nn.gelu(hid)
    o_ref[...] = _dot(act.astype(BF16), w2_ref[...]).astype(BF16)


def _compress(ck, cmp_pos_l, w1_l, w2_l, seq):
    n_chunk = seq // CMP_STRIDE
    feat = CMP_STRIDE * HEAD_DIM
    x = ck.reshape(4, n_chunk, feat)
    pos = cmp_pos_l.reshape(2, 2, feat)
    out = pl.pallas_call(
        _compress_kernel,
        out_shape=jax.ShapeDtypeStruct((4, n_chunk, HEAD_DIM), BF16),
        grid=(4,),
        in_specs=[
            pl.BlockSpec((None, n_chunk, feat), lambda g: (g, 0, 0)),
            pl.BlockSpec((None, 2, feat), lambda g: (g // 2, 0, 0)),
            pl.BlockSpec((None, 2 * feat, CMP_HIDDEN), lambda g: (g // 2, 0, 0)),
            pl.BlockSpec((None, CMP_HIDDEN, HEAD_DIM), lambda g: (g // 2, 0, 0)),
        ],
        out_specs=pl.BlockSpec((None, n_chunk, HEAD_DIM), lambda g: (g, 0, 0)),
        compiler_params=_cparams(("arbitrary",)),
        name="compress_tokens",
    )(x, pos, w1_l, w2_l)
    kc = jnp.concatenate([out[0], out[1]], axis=1)
    vct = out[2:4].reshape(2, n_chunk // BLOCK, BLOCK, HEAD_DIM).transpose(0, 1, 3, 2)
    return kc, vct


def _cmp_attn_kernel(qt_ref, kc_ref, vct_ref, mt_ref, o_ref, mq_ref, e_s, impsel_s, *, n_cmp, n_blk):
    kvh = pl.program_id(0)
    qi = pl.program_id(1)
    row_half = lax.broadcasted_iota(jnp.int32, (LANES, BLOCK), 0) >> 6
    qts = []
    for h in range(4):
        qh = qt_ref[h * HEAD_DIM:(h + 1) * HEAD_DIM, :].astype(F32)
        both = jnp.concatenate([qh, qh], axis=0)
        qts.append(jnp.where(row_half == kvh, both, 0.0).astype(BF16))
    qt_pairs = [jnp.concatenate(qts[0:2], axis=1), jnp.concatenate(qts[2:4], axis=1)]

    krow = lax.broadcasted_iota(jnp.int32, (BLOCK, 2 * BLOCK), 0)
    qcol = lax.broadcasted_iota(jnp.int32, (BLOCK, 2 * BLOCK), 1) & (BLOCK - 1)
    qpos = qi * BLOCK + qcol
    mtl = mt_ref[...]

    def attend(nc):
        m_locs, l_locs, pvs = [], [], []
        for c in range(nc):
            r0 = c * BLOCK
            kcc = kc_ref[r0:r0 + BLOCK, :]
            vcc = vct_ref[c]
            valid = (r0 + krow) * CMP_STRIDE + (CMP_BLOCK - 1) <= qpos
            ml, ll, pl_ = [], [], []
            for pair in range(2):
                cols = slice(pair * 2 * BLOCK, (pair + 1) * 2 * BLOCK)
                s = jnp.where(valid, _dot(kcc, qt_pairs[pair]), NEG)
                m_loc = jnp.max(s, axis=0, keepdims=True)
                e = jnp.where(valid, jnp.exp(s - m_loc), 0.0)
                e_s[r0:r0 + BLOCK, cols] = e
                ml.append(m_loc)
                ll.append(jnp.sum(e, axis=0, keepdims=True))
                pl_.append(_dot(vcc, e.astype(BF16)))
            m_locs.append(ml)
            l_locs.append(ll)
            pvs.append(pl_)
        scales = [[None, None] for _ in range(nc)]
        for pair in range(2):
            m_fin = m_locs[0][pair]
            for c in range(1, nc):
                m_fin = jnp.maximum(m_fin, m_locs[c][pair])
            ws = [jnp.exp(m_locs[c][pair] - m_fin) for c in range(nc)]
            den = ws[0] * l_locs[0][pair]
            acc = ws[0] * pvs[0][pair]
            for c in range(1, nc):
                den = den + ws[c] * l_locs[c][pair]
                acc = acc + ws[c] * pvs[c][pair]
            inv_l = 1.0 / jnp.maximum(den, 1e-30)
            o = acc * inv_l
            o_ref[:, pair * LANES:(pair + 1) * LANES] = jnp.concatenate([o[:, 0:BLOCK], o[:, BLOCK:2 * BLOCK]],
                                                                        axis=0).T
            for c in range(nc):
                scales[c][pair] = ws[c] * inv_l
        impsel_s[...] = jnp.zeros(impsel_s.shape, F32)
        for c in range(nc):
            r0 = c * BLOCK
            pn = e_s[r0:r0 + BLOCK, :] * jnp.concatenate(scales[c], axis=1)
            imp = pn[:, 0:BLOCK] + pn[:, BLOCK:2 * BLOCK] + pn[:, 2 * BLOCK:3 * BLOCK] + pn[:, 3 * BLOCK:4 * BLOCK]
            hi = imp.astype(BF16)
            r1 = imp - hi.astype(F32)
            mid = r1.astype(BF16)
            lo = (r1 - mid.astype(F32)).astype(BF16)
            b0 = c * (BLOCK // 4)
            impsel_s[b0:b0 + 64, :] += _dot(mtl, hi) + _dot(mtl, mid) + _dot(mtl, lo)

    n_ch = n_cmp // BLOCK
    if n_ch >= 4:
        few = qi < (n_ch // 2) * CMP_STRIDE

        @pl.when(few)
        def _():
            attend(n_ch // 2)

        @pl.when(jnp.logical_not(few))
        def _():
            attend(n_ch)
    else:
        attend(n_ch)

    imp_sel = impsel_s[0:n_blk, :]
    blk = lax.broadcasted_iota(jnp.int32, (n_blk, BLOCK), 0)
    cur = (qi * BLOCK + lax.broadcasted_iota(jnp.int32, (n_blk, BLOCK), 1)) >> 6
    forced = (blk == 0) | (blk == cur) | (blk == cur - 1)
    causal = blk <= cur
    score = jnp.where(forced, -jnp.inf, jnp.where(causal, imp_sel, NEG))
    blk_f = blk.astype(F32)
    chosen = jnp.where(forced | (cur < SEL_TOPK), 1.0, 0.0)
    for _ in range(SEL_TOPK - 3):
        mx = jnp.max(score, axis=0, keepdims=True)
        first = jnp.min(jnp.where(score == mx, blk_f, 1e9), axis=0, keepdims=True)
        pick = blk_f == first
        chosen = jnp.where(pick, 1.0, chosen)
        score = jnp.where(pick, -jnp.inf, score)
    mq_ref[...] = jnp.where(causal & (chosen > 0.5), 0.0, NEG).astype(BF16)


def _importance_matrix():
    mt = np.zeros((64, BLOCK), np.float32)
    ratio = SEL_BLOCK // CMP_STRIDE
    for o in range(-(CMP_BLOCK // CMP_STRIDE - 1), ratio):
        w = max(0, min(SEL_BLOCK, o * CMP_STRIDE + CMP_BLOCK) - max(0, o * CMP_STRIDE)) / CMP_BLOCK
        for r in range(BLOCK // ratio + 1):
            k = r * ratio + o
            if 0 <= k < BLOCK:
                mt[r, k] = w
    return jnp.asarray(mt, dtype=BF16)


def _cmp_attn(qt, kc, vct, seq):
    n_cmp = seq // CMP_STRIDE
    n_blk = max(LANES, seq // SEL_BLOCK)
    n_ch = n_cmp // BLOCK
    mt = _importance_matrix()
    kern = functools.partial(_cmp_attn_kernel, n_cmp=n_cmp, n_blk=n_blk)
    return pl.pallas_call(
        kern,
        out_shape=(jax.ShapeDtypeStruct((seq, 512), F32),
                   jax.ShapeDtypeStruct((2, seq // BLOCK, n_blk, BLOCK), BF16)),
        grid=(2, seq // BLOCK),
        in_specs=[
            pl.BlockSpec((4 * HEAD_DIM, BLOCK), lambda h, i: (h, i)),
            pl.BlockSpec((n_cmp, LANES), lambda h, i: (0, 0)),
            pl.BlockSpec((None, n_ch, HEAD_DIM, BLOCK), lambda h, i: (h, 0, 0, 0)),
            pl.BlockSpec((64, BLOCK), lambda h, i: (0, 0)),
        ],
        out_specs=(pl.BlockSpec((BLOCK, 256), lambda h, i: (i, h)),
                   pl.BlockSpec((None, None, n_blk, BLOCK), lambda h, i: (h, i, 0, 0))),
        scratch_shapes=[pltpu.VMEM((n_cmp, 4 * BLOCK), F32),
                        pltpu.VMEM((n_blk + 64, BLOCK), F32)],
        compiler_params=_cparams(("arbitrary", "arbitrary")),
        name="cmp_attn_topk",
    )(qt, kc, vct, mt)


def _sel_kernel(qi_ref, kt_ref, qt_ref, mq_ref, k_ref, vt_ref, e_ref, tab_ref, b31_ref, o_ref,
                qaug, m_s, acc_s, s_s, *, n_e):
    kvh = pl.program_id(0)
    step = pl.program_id(1)
    qi = qi_ref[step]
    kt = kt_ref[step]
    q_per_sub = SEL_SUB // SEL_TQ
    win_steps = LANES * SEL_BLOCK // SEL_TK
    ncol = 4 * SEL_TQ

    @pl.when(kt == 0)
    def _():
        m_s[...] = jnp.full(m_s.shape, -3e38, F32)
        acc_s[...] = jnp.zeros(acc_s.shape, F32)

    @pl.when(kt % win_steps == 0)
    def _():
        row_half = lax.broadcasted_iota(jnp.int32, (LANES, SEL_TQ), 0) >> 6
        mq = mq_ref[...]
        for h in range(4):
            qh = qt_ref[h * HEAD_DIM:(h + 1) * HEAD_DIM, :].astype(F32)
            both = jnp.concatenate([qh, qh], axis=0)
            qaug[0:LANES, h * SEL_TQ:(h + 1) * SEL_TQ] = jnp.where(row_half == kvh, both, 0.0).astype(BF16)
            qaug[LANES:2 * LANES, h * SEL_TQ:(h + 1) * SEL_TQ] = mq

    sub_near0 = jnp.maximum(0, (qi - SEL_SLAB_HI + q_per_sub - 1) // q_per_sub)

    def run(with_bias):
        pcols = [slice(pair * 2 * SEL_TQ, (pair + 1) * 2 * SEL_TQ) for pair in range(2)]
        qas = [qaug[:, cols] for cols in pcols]
        m_runs = [m_s[:, cols] for cols in pcols]
        if with_bias:
            b31s = [jnp.concatenate([jnp.full((1, SEL_TQ), b31_ref[kvh * 4 + pair * 2 + hh], F32)
                                     for hh in range(2)], axis=1) for pair in range(2)]
            m_runs = [m + jnp.where(sub_near0 >= kt * SEL_NSUB, b, 0.0) for m, b in zip(m_runs, b31s)]
        colmax = [jnp.full((8, 2 * SEL_TQ), -3e38, F32) for _ in range(2)]
        n_chunk = SEL_TK // SEL_CHUNK

        def score_chunk(pair, idx):
            r0 = idx * SEL_CHUNK
            j, c = divmod(idx, SEL_SUB // SEL_CHUNK)
            sub = kt * SEL_NSUB + j
            is_far = sub < sub_near0
            kaug = jnp.concatenate([k_ref[r0:r0 + SEL_CHUNK, :],
                                    e_ref[sub % n_e, c * SEL_CHUNK:(c + 1) * SEL_CHUNK, :]], axis=1)
            s = _dot(kaug, qas[pair])
            if with_bias:
                far_shift = jnp.where(is_far, b31s[pair], 0.0)
                blocks = []
                for hh in range(2):
                    col = []
                    for tt in range(SEL_CHUNK // BLOCK):
                        t = c * (SEL_CHUNK // BLOCK) + tt
                        u = jnp.maximum(qi - q_per_sub * sub - t, SEL_SLAB_LO)
                        uidx = jnp.where(is_far, SEL_NSLAB - 1, u - SEL_SLAB_LO)
                        col.append(s[tt * BLOCK:(tt + 1) * BLOCK, hh * SEL_TQ:(hh + 1) * SEL_TQ]
                                   + tab_ref[pair * 2 + hh, uidx])
                    blocks.append(jnp.concatenate(col, axis=0))
                s = jnp.concatenate(blocks, axis=1) + far_shift
            s_s[r0:r0 + SEL_CHUNK, pcols[pair]] = s
            for g in range(SEL_CHUNK // 8):
                colmax[pair] = jnp.maximum(colmax[pair], s[g * 8:(g + 1) * 8, :])

        for idx in range(n_chunk):
            score_chunk(0, idx)
            score_chunk(1, idx)
        m_news = [jnp.maximum(m, jnp.max(cm, axis=0, keepdims=True)) for m, cm in zip(m_runs, colmax)]
        accs = [jnp.exp(m_runs[pair] - m_news[pair]) * acc_s[:, pcols[pair]] for pair in range(2)]
        for idx in range(n_chunk):
            r0 = idx * SEL_CHUNK
            vt = vt_ref[0:SEL_VROWS, r0:r0 + SEL_CHUNK]
            for pair in range(2):
                pr = jnp.exp(s_s[r0:r0 + SEL_CHUNK, pcols[pair]] - m_news[pair]).astype(BF16)
                accs[pair] = accs[pair] + _dot(vt, pr)
        for pair in range(2):
            acc_s[:, pcols[pair]] = accs[pair]
            m_s[:, pcols[pair]] = m_news[pair]

    all_far = kt * SEL_NSUB + SEL_NSUB - 1 < sub_near0

    @pl.when(all_far)
    def _():
        run(False)

    @pl.when(jnp.logical_not(all_far))
    def _():
        run(True)

    @pl.when(kt == qi // (SEL_TK // SEL_TQ))
    def _():
        acc = acc_s[...]
        o = acc[0:HEAD_DIM, :] / acc[HEAD_DIM:HEAD_DIM + 1, :]
        for jp in range(2):
            blk = jnp.concatenate([o[:, (2 * jp) * SEL_TQ:(2 * jp + 1) * SEL_TQ],
                                   o[:, (2 * jp + 1) * SEL_TQ:(2 * jp + 2) * SEL_TQ]], axis=0)
            o_ref[:, jp * LANES:(jp + 1) * LANES] = blk.T


def _sel_attn(qkv, qt, vt, mq, bias_s, rel_bias, seq):
    n_q = seq // SEL_TQ
    per_step = SEL_TK // SEL_TQ
    qi_l, kt_l = [], []
    for qi in range(n_q):
        for kt in range(qi // per_step + 1):
            qi_l.append(qi)
            kt_l.append(kt)
    qi_arr = jnp.asarray(np.array(qi_l, np.int32))
    kt_arr = jnp.asarray(np.array(kt_l, np.int32))
    win_steps = LANES * SEL_BLOCK // SEL_TK
    n_e = min(LANES * SEL_BLOCK // SEL_SUB, seq // SEL_SUB)
    e_np = np.zeros((n_e, SEL_SUB, LANES), np.float32)
    for t in range(n_e):
        for k in range(SEL_SUB):
            e_np[t, k, (t * (SEL_SUB // SEL_BLOCK) + k // SEL_BLOCK) % LANES] = 1.0
    e_all = jnp.asarray(e_np, dtype=BF16)
    b31 = rel_bias[REL_BUCKETS - 1, 4:12]
    grid_spec = pltpu.PrefetchScalarGridSpec(
        num_scalar_prefetch=2,
        grid=(2, len(qi_l)),
        in_specs=[
            pl.BlockSpec((4 * HEAD_DIM, SEL_TQ), lambda h, s, qi, kt: (h, qi[s])),
            pl.BlockSpec((None, None, LANES, SEL_TQ), lambda h, s, qi, kt: (h, qi[s], kt[s] // win_steps, 0)),
            pl.BlockSpec((SEL_TK, LANES), lambda h, s, qi, kt: (kt[s], _qkv_col(COL_CKS) // LANES)),
            pl.BlockSpec((None, LANES, SEL_TK), lambda h, s, qi, kt: (h, 0, kt[s])),
            pl.BlockSpec((n_e, SEL_SUB, LANES), lambda h, s, qi, kt: (0, 0, 0)),
            pl.BlockSpec((4, SEL_NSLAB, BLOCK, BLOCK), lambda h, s, qi, kt: (h, 0, 0, 0)),
            pl.BlockSpec(memory_space=pltpu.SMEM),
        ],
        out_specs=pl.BlockSpec((SEL_TQ, 256), lambda h, s, qi, kt: (qi[s], h)),
        scratch_shapes=[pltpu.VMEM((2 * LANES, 4 * SEL_TQ), BF16),
                        pltpu.VMEM((1, 4 * SEL_TQ), F32),
                        pltpu.VMEM((SEL_VROWS, 4 * SEL_TQ), F32),
                        pltpu.VMEM((SEL_TK, 4 * SEL_TQ), F32)],
    )
    return pl.pallas_call(
        functools.partial(_sel_kernel, n_e=n_e),
        out_shape=jax.ShapeDtypeStruct((seq, 512), F32),
        grid_spec=grid_spec,
        compiler_params=_cparams(("arbitrary", "arbitrary")),
        name="selected_attn",
    )(qi_arr, kt_arr, qt, mq, qkv, vt, e_all, bias_s, b31)


def _layer_norm(z, g, b):
    mu = jnp.mean(z, axis=-1, keepdims=True)
    zc = z - mu
    var = jnp.mean(jnp.square(zc), axis=-1, keepdims=True)
    return zc * lax.rsqrt(var + LN_EPS) * g + b


def _outproj_kernel(x_ref, oa0, oa1, oa2, la0, la1, la2, ob_ref, ocmp_ref, oslc_ref, owin_ref, g_ref,
                    gx_ref, w_ref, ga_ref, lng_ref, lnb_ref, o_ref):
    l0, l1, l2 = la0[...], la1[...], la2[...]
    mx = jnp.maximum(jnp.maximum(l0, l1), l2)
    w0, w1, w2 = jnp.exp(l0 - mx), jnp.exp(l1 - mx), jnp.exp(l2 - mx)
    oa = (w0 * oa0[...] + w1 * oa1[...] + w2 * oa2[...]) / (w0 + w1 + w2)
    gt = jax.nn.sigmoid(g_ref[...])
    hi = gt.astype(BF16)
    lo = (gt - hi.astype(F32)).astype(BF16)
    gx = gx_ref[...]
    gates = _dot(hi, gx) + _dot(lo, gx)
    oc = gates[:, 0:512] * ocmp_ref[...] + gates[:, 512:1024] * oslc_ref[...] + gates[:, 1024:1536] * owin_ref[...]
    mixed = jnp.concatenate([oa, ob_ref[...], oc], axis=1).astype(BF16)
    y = _dot(mixed, w_ref[...])
    z = DN_ALPHA * x_ref[...] + (1 + ga_ref[...]) * y
    o_ref[...] = _layer_norm(z, lng_ref[...], lnb_ref[...])


def _gate_expand():
    gx = np.zeros((LANES, 3 * 512), np.float32)
    for h in range(8):
        for b in range(3):
            gx[h * 3 + b, b * 512 + h * HEAD_DIM:b * 512 + (h + 1) * HEAD_DIM] = 1.0
    return jnp.asarray(gx, dtype=BF16)


def _outproj(x, oa, la, ob, ocmp, oslc, owin, gates, w_out_l, mod_l, lng, lnb, seq):
    tm = 256
    row = lambda w: pl.BlockSpec((tm, w), lambda i: (i, 0))
    vec = lambda k: pl.BlockSpec((1, D_MODEL), lambda i, k=k: (0, k))
    return pl.pallas_call(
        _outproj_kernel,
        out_shape=jax.ShapeDtypeStruct((seq, D_MODEL), F32),
        grid=(seq // tm,),
        in_specs=[row(D_MODEL)] + [row(256)] * 7 + [row(512)] * 3 + [row(LANES)] + [
            pl.BlockSpec((LANES, 3 * 512), lambda i: (0, 0)),
            pl.BlockSpec((D_MODEL, D_MODEL), lambda i: (0, 0)),
            vec(2), vec(0), vec(0)],
        out_specs=row(D_MODEL),
        compiler_params=_cparams(("arbitrary",)),
        name="out_proj_ln",
    )(x, oa[0], oa[1], oa[2], la[0], la[1], la[2], ob, ocmp, oslc, owin, gates, _gate_expand(), w_out_l,
      mod_l, lng, lnb)


def _mlp_kernel(x_ref, sc_ref, sh_ref, g_ref, w1_ref, w2_ref, lng_ref, lnb_ref, o_ref, h_s, acc_s):
    j = pl.program_id(1)

    @pl.when(j == 0)
    def _():
        h_s[...] = (x_ref[...] * (1 + sc_ref[...]) + sh_ref[...]).astype(BF16)
        acc_s[...] = jnp.zeros(acc_s.shape, F32)

    f = jnp.maximum(_dot(h_s[...], w1_ref[...]), 0.0)
    acc_s[...] += _dot(jnp.square(f).astype(BF16), w2_ref[...])

    @pl.when(j == pl.num_programs(1) - 1)
    def _():
        z = DN_ALPHA * x_ref[...] + (1 + g_ref[...]) * acc_s[...]
        o_ref[...] = _layer_norm(z, lng_ref[...], lnb_ref[...])


def _mlp(x, mod_l, w1_l, w2_l, lng, lnb, seq):
    tm = 1024
    tf = 1024
    vec = lambda k: pl.BlockSpec((1, D_MODEL), lambda i, j, k=k: (0, k))
    return pl.pallas_call(
        _mlp_kernel,
        out_shape=jax.ShapeDtypeStruct((seq, D_MODEL), F32),
        grid=(seq // tm, D_FF // tf),
        in_specs=[
            pl.BlockSpec((tm, D_MODEL), lambda i, j: (i, 0)),
            vec(4), vec(3), vec(5),
            pl.BlockSpec((D_MODEL, tf), lambda i, j: (0, j)),
            pl.BlockSpec((tf, D_MODEL), lambda i, j: (j, 0)),
            vec(0), vec(0),
        ],
        out_specs=pl.BlockSpec((tm, D_MODEL), lambda i, j: (i, 0)),
        scratch_shapes=[pltpu.VMEM((tm, D_MODEL), BF16), pltpu.VMEM((tm, D_MODEL), F32)],
        compiler_params=_cparams(("arbitrary", "arbitrary")),
        name="mlp_ln",
    )(x, mod_l, mod_l, mod_l, w1_l, w2_l, lng, lnb)


def kernel(x, c, positions, w_in, w_out, rel_bias, sinks, cmp_pos, cmp_w1, cmp_w2, mlp_w1, mlp_w2,
           ada_w, ada_b, ln_g, ln_b):
    bsz, seq, d = x.shape
    assert bsz == 1 and d == D_MODEL and seq % (2 * A_CONFIGS[-1][0]) == 0
    x = x.reshape(seq, D_MODEL)

    mod = _modulation(c, ada_w, ada_b)
    cos_t, sin_t = _rope_tables(positions, seq)
    bias_a, bias_w, bias_s = _bias_tables(rel_bias)

    w_in_b = jnp.pad(w_in, ((0, 0), (0, 0), (0, N_IN_PAD - N_IN))).astype(BF16)
    w_out_b = w_out.astype(BF16)
    w1_b = mlp_w1.astype(BF16)
    w2_b = mlp_w2.astype(BF16)
    cw1_b = cmp_w1.astype(BF16)
    cw2_b = cmp_w2.astype(BF16)

    for l in range(DEPTH):
        mod_l = mod[l]
        qkv, ck, gates, qt, vt, a_qkv = _inproj(x, mod_l, w_in_b[l], cos_t, sin_t, seq)

        oa, la = [], []
        for ci, (window, dil) in enumerate(A_CONFIGS):
            nrows = seq // dil
            view = a_qkv.reshape(nrows, dil * N_A)
            o_c, l_c = _banded_call(
                view, nrows, dil, N_A // LANES, 0, 2, 4, 2,
                nprev=1, tq=min(BAND_TQ, nrows), max_dist=window // dil, bias=bias_a[ci], want_lse=True,
                group_div=None, kv_per_qblock=True, name=f"dilated_attn_{dil}")
            oa.append(o_c.reshape(seq, 256))
            la.append(l_c.reshape(seq, 256))

        (ob,) = _banded_call(
            qkv, seq, 1, N_QKV // LANES, _qkv_col(COL_BQ) // LANES, _qkv_col(COL_BK) // LANES,
            _qkv_col(COL_BV) // LANES, 2,
            nprev=1, tq=BAND_TQ, max_dist=B_WINDOW - 1, sinks=sinks[l], group_div=1,
            kv_per_qblock=False, name="swa_attn")

        kc, vct = _compress(ck, cmp_pos[l], cw1_b[l], cw2_b[l], seq)
        ocmp, mq = _cmp_attn(qt, kc, vct, seq)
        oslc = _sel_attn(qkv, qt, vt, mq, bias_s, rel_bias, seq)
        (owin,) = _banded_call(
            qkv, seq, 1, N_QKV // LANES, _qkv_col(COL_CQ) // LANES, _qkv_col(COL_CKW) // LANES,
            _qkv_col(COL_CVW) // LANES, 4,
            nprev=C_WINDOW // BLOCK, tq=BAND_TQ, max_dist=C_WINDOW - 1, bias=bias_w, group_div=2,
            kv_per_qblock=False, name="window_attn")

        x = _outproj(x, oa, la, ob, ocmp, oslc, owin, gates, w_out_b[l], mod_l,
                     ln_g[l, 0:1], ln_b[l, 0:1], seq)
        x = _mlp(x, mod_l, w1_b[l], w2_b[l], ln_g[l, 1:2], ln_b[l, 1:2], seq)

    return x.reshape(bsz, seq, D_MODEL)
```

```python
import functools
import math

import numpy as np
import jax
import jax.numpy as jnp
from jax import lax
from jax.experimental import pallas as pl
from jax.experimental.pallas import tpu as pltpu

F32 = jnp.float32
BF16 = jnp.bfloat16

D_MODEL = 1024
DEPTH = 4
HEAD_DIM = 64
LANES = 128
BLOCK = 128
A_CONFIGS = ((128, 1), (512, 4), (2048, 16))
B_WINDOW = 128
ROPE_THETA = 150000.0
CMP_BLOCK = 32
CMP_STRIDE = 16
CMP_HIDDEN = 256
SEL_BLOCK = 64
SEL_TOPK = 16
C_WINDOW = 512
REL_BUCKETS = 32
REL_MAX_DIST = 2048
D_FF = 4 * D_MODEL
DN_ALPHA = (2 * DEPTH) ** 0.25
LN_EPS = 1e-5
NEG = -1e30
FORCE = 1e4
SCALE = HEAD_DIM ** -0.5

COL_AQ, COL_AK, COL_AV = 0, 256, 512
COL_BQ, COL_BK, COL_BV = 768, 1024, 1152
COL_CQ = 1280
COL_CKC = 1792
COL_CKS, COL_CVS = 2048, 2176
COL_CKW, COL_CVW = 2304, 2432
COL_CG = 2560
N_A = 768
N_QKV = 1536


def _qkv_col(col):
    assert COL_BQ <= col < COL_CKC or COL_CKS <= col < COL_CG
    return col - N_A if col < COL_CKC else col - N_A - 256
N_IN = 2584
N_IN_PAD = 2688

BAND_TQ = 1024
SEL_TQ = 128
SEL_SUB = 512
SEL_CHUNK = 128
SEL_VROWS = 80
SEL_NSUB = 4
SEL_TK = SEL_SUB * SEL_NSUB
SEL_SLAB_LO = -1
SEL_SLAB_HI = 15
SEL_NSLAB = SEL_SLAB_HI - SEL_SLAB_LO + 2
VMEM_LIMIT = 56 * 1024 * 1024


def _cparams(sem, flags=None):
    return pltpu.CompilerParams(dimension_semantics=sem, vmem_limit_bytes=VMEM_LIMIT, flags=flags)


def _dot_nt(a, b):
    return lax.dot_general(a, b, (((1,), (1,)), ((), ())), preferred_element_type=F32)


def _dot(a, b):
    return jnp.dot(a, b, preferred_element_type=F32)


def _mod_kernel(c_ref, w_ref, b_ref, o_ref):
    c = c_ref[...]
    act = c * jax.nn.sigmoid(c)
    lhs = jnp.broadcast_to(act, (8, D_MODEL)).astype(BF16)
    y = _dot(lhs, w_ref[...].astype(BF16))
    o_ref[...] = y[0:1, :] + b_ref[...]


def _modulation(c, ada_w, ada_b):
    tn = 1536
    return pl.pallas_call(
        _mod_kernel,
        out_shape=jax.ShapeDtypeStruct((DEPTH, 1, 6 * D_MODEL), F32),
        grid=(DEPTH, 6 * D_MODEL // tn),
        in_specs=[
            pl.BlockSpec((1, D_MODEL), lambda l, j: (0, 0)),
            pl.BlockSpec((None, D_MODEL, tn), lambda l, j: (l, 0, j)),
            pl.BlockSpec((None, 1, tn), lambda l, j: (l, 0, j)),
        ],
        out_specs=pl.BlockSpec((None, 1, tn), lambda l, j: (l, 0, j)),
        compiler_params=_cparams(("arbitrary", "arbitrary")),
        name="adaln_mod",
    )(c, ada_w, ada_b.reshape(DEPTH, 1, 6 * D_MODEL))


def _rope_table_kernel(pos_ref, freq_ref, cos_ref, sin_ref):
    ang = pos_ref[...].astype(F32) * freq_ref[...]
    cos_ref[...] = jnp.cos(ang)
    sin_ref[...] = jnp.sin(ang)


def _rope_tables(positions, seq):
    half = HEAD_DIM // 2
    freq = ROPE_THETA ** (-jnp.arange(half, dtype=F32) / half)
    freq = jnp.tile(freq, LANES // half).reshape(1, LANES)
    tm = 1024
    return pl.pallas_call(
        _rope_table_kernel,
        out_shape=(jax.ShapeDtypeStruct((seq, LANES), F32),) * 2,
        grid=(seq // tm,),
        in_specs=[pl.BlockSpec((tm, 1), lambda i: (i, 0)), pl.BlockSpec((1, LANES), lambda i: (0, 0))],
        out_specs=(pl.BlockSpec((tm, LANES), lambda i: (i, 0)),) * 2,
        compiler_params=_cparams(("arbitrary",)),
        name="rope_tables",
    )(positions.reshape(seq, 1), freq)


def _t5_bias(rb_ref, dist, col, max_dist):
    exact = REL_BUCKETS // 2
    d = jnp.maximum(dist, 0)
    df = jnp.maximum(d, 1).astype(F32)
    large = exact + (jnp.log(df / exact) / math.log(REL_MAX_DIST / exact)
                     * (REL_BUCKETS - exact)).astype(jnp.int32)
    large = jnp.minimum(large, REL_BUCKETS - 1)
    bucket = jnp.where(d < exact, d, large)
    out = jnp.zeros(dist.shape, F32)
    for b in range(REL_BUCKETS):
        out = jnp.where(bucket == b, rb_ref[b * 12 + col], out)
    return out, bucket


def _bias_a_kernel(rb_ref, o_ref):
    cfg = pl.program_id(0)
    head = pl.program_id(1)
    dil = jnp.where(cfg == 0, 1, jnp.where(cfg == 1, 4, 16))
    r = lax.broadcasted_iota(jnp.int32, (BLOCK, 2 * BLOCK), 0)
    c = lax.broadcasted_iota(jnp.int32, (BLOCK, 2 * BLOCK), 1)
    dist = r + BLOCK - c
    bias, _ = _t5_bias(rb_ref, dist * dil, head, None)
    o_ref[...] = jnp.where((dist >= 0) & (dist <= BLOCK), bias, NEG)


def _bias_w_kernel(rb_ref, o_ref):
    head = pl.program_id(0)
    nprev = C_WINDOW // BLOCK
    r = lax.broadcasted_iota(jnp.int32, (BLOCK, (nprev + 1) * BLOCK), 0)
    c = lax.broadcasted_iota(jnp.int32, (BLOCK, (nprev + 1) * BLOCK), 1)
    dist = r + nprev * BLOCK - c
    bias, _ = _t5_bias(rb_ref, dist, 4 + head, None)
    o_ref[...] = jnp.where((dist >= 0) & (dist <= C_WINDOW - 1), bias, NEG)


def _bias_s_kernel(rb_ref, o_ref):
    head = pl.program_id(0)
    slab = pl.program_id(1)
    r = lax.broadcasted_iota(jnp.int32, (BLOCK, BLOCK), 1)
    c = lax.broadcasted_iota(jnp.int32, (BLOCK, BLOCK), 0)
    dist = (slab + SEL_SLAB_LO) * BLOCK + r - c
    bias, _ = _t5_bias(rb_ref, dist, 4 + head, None)
    val = jnp.where(dist >= 0, bias, NEG)
    o_ref[...] = jnp.where(slab == SEL_NSLAB - 1, 0.0, val)


def _bias_tables(rel_bias):
    rb = rel_bias.reshape(-1)
    smem = pl.BlockSpec(memory_space=pltpu.SMEM)
    bias_a = pl.pallas_call(
        _bias_a_kernel,
        out_shape=jax.ShapeDtypeStruct((3, 4, BLOCK, 2 * BLOCK), F32),
        grid=(3, 4),
        in_specs=[smem],
        out_specs=pl.BlockSpec((None, None, BLOCK, 2 * BLOCK), lambda a, h: (a, h, 0, 0)),
        compiler_params=_cparams(("arbitrary", "arbitrary")),
        name="bias_table_dilated",
    )(rb)
    wctx = C_WINDOW + BLOCK
    bias_w = pl.pallas_call(
        _bias_w_kernel,
        out_shape=jax.ShapeDtypeStruct((8, BLOCK, wctx), F32),
        grid=(8,),
        in_specs=[smem],
        out_specs=pl.BlockSpec((None, BLOCK, wctx), lambda h: (h, 0, 0)),
        compiler_params=_cparams(("arbitrary",)),
        name="bias_table_window",
    )(rb)
    bias_s = pl.pallas_call(
        _bias_s_kernel,
        out_shape=jax.ShapeDtypeStruct((8, SEL_NSLAB, BLOCK, BLOCK), F32),
        grid=(8, SEL_NSLAB),
        in_specs=[smem],
        out_specs=pl.BlockSpec((None, None, BLOCK, BLOCK), lambda h, u: (h, u, 0, 0)),
        compiler_params=_cparams(("arbitrary", "arbitrary")),
        name="bias_table_selected",
    )(rb)
    return bias_a, bias_w, bias_s


def _rope_apply(t, cos, sin, first):
    up = pltpu.roll(t, LANES - HEAD_DIM // 2, axis=1)
    dn = pltpu.roll(t, HEAD_DIM // 2, axis=1)
    return t * cos + jnp.where(first, -up, dn) * sin


def _inproj_kernel(x_ref, sc_ref, sh_ref, w_ref, cos_ref, sin_ref, qkv_ref, ck_ref, g_ref, qt_ref, vt_ref,
                   a_ref):
    h = (x_ref[...] * (1 + sc_ref[...]) + sh_ref[...]).astype(BF16)
    y = _dot(h, w_ref[...])
    cos = cos_ref[...]
    sin = sin_ref[...]
    lane = lax.broadcasted_iota(jnp.int32, cos.shape, 1)
    first = (lane & (HEAD_DIM - 1)) < HEAD_DIM // 2

    def put(c0, width, val):
        q0 = _qkv_col(c0)
        qkv_ref[:, q0:q0 + width] = val.astype(BF16)

    a_ref[:, 0:256] = (y[:, COL_AQ:COL_AQ + 256] * SCALE).astype(BF16)
    a_ref[:, 256:768] = y[:, COL_AK:COL_AK + 512].astype(BF16)
    for j in range(2):
        c0 = COL_BQ + j * LANES
        put(c0, LANES, _rope_apply(y[:, c0:c0 + LANES], cos, sin, first) * SCALE)
    put(COL_BK, LANES, _rope_apply(y[:, COL_BK:COL_BK + LANES], cos, sin, first))
    put(COL_BV, LANES, y[:, COL_BV:COL_BV + LANES])
    put(COL_CQ, 512, y[:, COL_CQ:COL_CQ + 512] * SCALE)
    put(COL_CKS, 512, y[:, COL_CKS:COL_CKS + 512])
    for g in range(4):
        c0 = COL_CKC + g * HEAD_DIM
        ck_ref[g] = y[:, c0:c0 + HEAD_DIM]
    g_ref[...] = y[:, COL_CG:COL_CG + LANES]
    qt_ref[...] = (y[:, COL_CQ:COL_CQ + 512] * SCALE).T.astype(BF16)
    vt = y[:, COL_CVS:COL_CVS + LANES].T
    ones_row = (lax.broadcasted_iota(jnp.int32, (HEAD_DIM, vt.shape[1]), 0) == 0).astype(F32)
    for kvh in range(2):
        vt_ref[kvh] = jnp.concatenate([vt[kvh * HEAD_DIM:(kvh + 1) * HEAD_DIM, :], ones_row], axis=0).astype(BF16)


def _inproj(x, mod_l, w_in_l, cos_t, sin_t, seq):
    tm = 512
    vec = lambda k: pl.BlockSpec((1, D_MODEL), lambda i, k=k: (0, k))
    return pl.pallas_call(
        _inproj_kernel,
        out_shape=(jax.ShapeDtypeStruct((seq, N_QKV), BF16),
                   jax.ShapeDtypeStruct((4, seq, HEAD_DIM), F32),
                   jax.ShapeDtypeStruct((seq, LANES), F32),
                   jax.ShapeDtypeStruct((512, seq), BF16),
                   jax.ShapeDtypeStruct((2, LANES, seq), BF16),
                   jax.ShapeDtypeStruct((seq, N_A), BF16)),
        grid=(seq // tm,),
        in_specs=[
            pl.BlockSpec((tm, D_MODEL), lambda i: (i, 0)),
            vec(1), vec(0),
            pl.BlockSpec((D_MODEL, N_IN_PAD), lambda i: (0, 0)),
            pl.BlockSpec((tm, LANES), lambda i: (i, 0)),
            pl.BlockSpec((tm, LANES), lambda i: (i, 0)),
        ],
        out_specs=(pl.BlockSpec((tm, N_QKV), lambda i: (i, 0)),
                   pl.BlockSpec((4, tm, HEAD_DIM), lambda i: (0, i, 0)),
                   pl.BlockSpec((tm, LANES), lambda i: (i, 0)),
                   pl.BlockSpec((512, tm), lambda i: (0, i)),
                   pl.BlockSpec((2, LANES, tm), lambda i: (0, 0, i)),
                   pl.BlockSpec((tm, N_A), lambda i: (i, 0))),
        compiler_params=_cparams(("arbitrary",)),
        name="in_proj",
    )(x, mod_l, mod_l, w_in_l, cos_t, sin_t)


def _lane_half():
    return lax.broadcasted_iota(jnp.int32, (BLOCK, LANES), 1) >> 6


def _align_q(q, half, x, ysel):
    qx = jnp.where(half == x, q, 0.0)
    if ysel is None:
        return qx.astype(BF16)
    qd = qx + pltpu.roll(qx, HEAD_DIM, axis=1)
    return jnp.where(ysel, qd, 0.0).astype(BF16)


def _spread_o(o, ysel):
    if ysel is None:
        return o
    ob = jnp.where(ysel, o, 0.0)
    return ob + pltpu.roll(ob, HEAD_DIM, axis=1)


def _banded_kernel(*refs, nprev, tq, max_dist, use_bias, use_sinks, want_lse, group_div, row_axis):
    refs = list(refs)
    sink_ref = refs.pop(0) if use_sinks else None
    q_ref, kp_ref, kc_ref, vp_ref, vc_ref = refs[:5]
    refs = refs[5:]
    bias_ref = refs.pop(0) if use_bias else None
    o_ref = refs.pop(0)
    lse_ref = refs.pop(0) if want_lse else None
    kctx, vctx = refs

    tp = nprev * BLOCK
    ctx = (nprev + 1) * BLOCK
    i = pl.program_id(row_axis)
    p = pl.program_id(row_axis - 1)
    kctx[0:tp, :] = kp_ref[...]
    kctx[tp:tp + tq, :] = kc_ref[...]
    vctx[0:tp, :] = vp_ref[...]
    vctx[tp:tp + tq, :] = vc_ref[...]

    half = _lane_half()
    upper = half == 1
    ysel = None if group_div is None else half == p // group_div
    chunks = [(c0, min(c0 + 2 * BLOCK, ctx)) for c0 in range(0, ctx, 2 * BLOCK)]
    rrs, ccs, bands = [], [], []
    for c0, c1 in chunks:
        rr = lax.broadcasted_iota(jnp.int32, (BLOCK, c1 - c0), 0)
        cc = lax.broadcasted_iota(jnp.int32, (BLOCK, c1 - c0), 1) + c0
        ccs.append(cc)
        if not use_bias:
            dist = rr + tp - cc
            bands.append((dist >= 0) & (dist <= max_dist))

    for sub in range(tq // BLOCK):
        rows = slice(sub * BLOCK, (sub + 1) * BLOCK)
        q = q_ref[rows, :].astype(F32)
        low = tp - (i * tq + sub * BLOCK)
        outs, lses = [], []
        for x in range(2):
            qa = _align_q(q, half, x, ysel)
            ms, dens, pvs = [], [], []
            for ci, (c0, c1) in enumerate(chunks):
                k = kctx[sub * BLOCK + c0:sub * BLOCK + c1, :]
                v = vctx[sub * BLOCK + c0:sub * BLOCK + c1, :]
                s = _dot_nt(qa, k)
                if use_bias:
                    s = s + bias_ref[x, :, c0:c1]
                else:
                    s = jnp.where(bands[ci], s, NEG)
                s = jnp.where(ccs[ci] >= low, s, NEG)
                m_loc = jnp.max(s, axis=1, keepdims=True)
                e = jnp.exp(s - m_loc)
                ms.append(m_loc)
                dens.append(jnp.sum(e, axis=1, keepdims=True))
                pvs.append(_dot(e.astype(BF16), v))
            m = ms[0]
            for m_loc in ms[1:]:
                m = jnp.maximum(m, m_loc)
            if use_sinks:
                sk = sink_ref[2 * p + x]
                m = jnp.maximum(m, sk)
            ws = [jnp.exp(m_loc - m) for m_loc in ms]
            den = ws[0] * dens[0]
            acc = ws[0] * pvs[0]
            for w, d, pv in zip(ws[1:], dens[1:], pvs[1:]):
                den = den + w * d
                acc = acc + w * pv
            if use_sinks:
                den = den + jnp.exp(sk - m)
            o = acc / den
            outs.append(_spread_o(o, ysel))
            lses.append(m + jnp.log(den))
        o_ref[rows, :] = jnp.where(upper, outs[1], outs[0])
        if want_lse:
            lse_ref[rows, :] = jnp.where(upper, jnp.broadcast_to(lses[1], (BLOCK, LANES)),
                                         jnp.broadcast_to(lses[0], (BLOCK, LANES)))


def _banded_call(qkv_view, nrows, nsub, row_width_blocks, q_cb, k_cb, v_cb, n_qblocks, *, nprev, tq,
                 max_dist, bias=None, sinks=None, want_lse=False, group_div=None, kv_per_qblock, name):
    tp = nprev * BLOCK
    ratio = tq // tp
    ctx = (nprev + 1) * BLOCK

    def kvb(p):
        return p if kv_per_qblock else 0

    in_specs = []
    args = []
    if sinks is not None:
        in_specs.append(pl.BlockSpec(memory_space=pltpu.SMEM))
        args.append(sinks)
    in_specs += [
        pl.BlockSpec((tq, LANES), lambda r, p, i: (i, r * row_width_blocks + q_cb + p)),
        pl.BlockSpec((tp, LANES), lambda r, p, i: (jnp.maximum(i * ratio - 1, 0), r * row_width_blocks + k_cb + kvb(p))),
        pl.BlockSpec((tq, LANES), lambda r, p, i: (i, r * row_width_blocks + k_cb + kvb(p))),
        pl.BlockSpec((tp, LANES), lambda r, p, i: (jnp.maximum(i * ratio - 1, 0), r * row_width_blocks + v_cb + kvb(p))),
        pl.BlockSpec((tq, LANES), lambda r, p, i: (i, r * row_width_blocks + v_cb + kvb(p))),
    ]
    args += [qkv_view] * 5
    if bias is not None:
        in_specs.append(pl.BlockSpec((2, BLOCK, ctx), lambda r, p, i: (p, 0, 0)))
        args.append(bias)
    out_w = nsub * n_qblocks * LANES
    o_spec = pl.BlockSpec((tq, LANES), lambda r, p, i: (i, r * n_qblocks + p))
    out_shape = [jax.ShapeDtypeStruct((nrows, out_w), F32)]
    out_specs = [o_spec]
    if want_lse:
        out_shape.append(jax.ShapeDtypeStruct((nrows, out_w), F32))
        out_specs.append(o_spec)
    kern = functools.partial(_banded_kernel, nprev=nprev, tq=tq, max_dist=max_dist, use_bias=bias is not None,
                             use_sinks=sinks is not None, want_lse=want_lse, group_div=group_div, row_axis=2)
    return pl.pallas_call(
        kern,
        out_shape=tuple(out_shape),
        grid=(nsub, n_qblocks, nrows // tq),
        in_specs=in_specs,
        out_specs=tuple(out_specs),
        scratch_shapes=[pltpu.VMEM((tp + tq, LANES), BF16), pltpu.VMEM((tp + tq, LANES), BF16)],
        compiler_params=_cparams(("arbitrary", "arbitrary", "arbitrary")),
        name=name,
    )(*args)


def _compress_kernel(x_ref, pos_ref, w1_ref, w2_ref, o_ref):
    x = x_ref[...]
    half = CMP_STRIDE * HEAD_DIM
    xa = (x + pos_ref[0:1, :]).astype(BF16)
    xb = (x + pos_ref[1:2, :]).astype(BF16)
    first = _dot(xa, w1_ref[0:half, :])
    second = _dot(xb, w1_ref[half:2 * half, :])
    n = x.shape[0]
    hid = first + pltpu.roll(second, n - 1, axis=0)
    act = jax.nn.gelu(hid)
    o_ref[...] = _dot(act.astype(BF16), w2_ref[...]).astype(BF16)


def _compress(ck, cmp_pos_l, w1_l, w2_l, seq):
    n_chunk = seq // CMP_STRIDE
    feat = CMP_STRIDE * HEAD_DIM
    x = ck.reshape(4, n_chunk, feat)
    pos = cmp_pos_l.reshape(2, 2, feat)
    out = pl.pallas_call(
        _compress_kernel,
        out_shape=jax.ShapeDtypeStruct((4, n_chunk, HEAD_DIM), BF16),
        grid=(4,),
        in_specs=[
            pl.BlockSpec((None, n_chunk, feat), lambda g: (g, 0, 0)),
            pl.BlockSpec((None, 2, feat), lambda g: (g // 2, 0, 0)),
            pl.BlockSpec((None, 2 * feat, CMP_HIDDEN), lambda g: (g // 2, 0, 0)),
            pl.BlockSpec((None, CMP_HIDDEN, HEAD_DIM), lambda g: (g // 2, 0, 0)),
        ],
        out_specs=pl.BlockSpec((None, n_chunk, HEAD_DIM), lambda g: (g, 0, 0)),
        compiler_params=_cparams(("arbitrary",)),
        name="compress_tokens",
    )(x, pos, w1_l, w2_l)
    kc = jnp.concatenate([out[0], out[1]], axis=1)
    vct = out[2:4].reshape(2, n_chunk // BLOCK, BLOCK, HEAD_DIM).transpose(0, 1, 3, 2)
    return kc, vct


def _cmp_attn_kernel(qt_ref, kc_ref, vct_ref, mt_ref, o_ref, mq_ref, e_s, impsel_s, *, n_cmp, n_blk):
    kvh = pl.program_id(0)
    qi = pl.program_id(1)
    row_half = lax.broadcasted_iota(jnp.int32, (LANES, BLOCK), 0) >> 6
    qts = []
    for h in range(4):
        qh = qt_ref[h * HEAD_DIM:(h + 1) * HEAD_DIM, :].astype(F32)
        both = jnp.concatenate([qh, qh], axis=0)
        qts.append(jnp.where(row_half == kvh, both, 0.0).astype(BF16))
    qt_pairs = [jnp.concatenate(qts[0:2], axis=1), jnp.concatenate(qts[2:4], axis=1)]

    krow = lax.broadcasted_iota(jnp.int32, (BLOCK, 2 * BLOCK), 0)
    qcol = lax.broadcasted_iota(jnp.int32, (BLOCK, 2 * BLOCK), 1) & (BLOCK - 1)
    qpos = qi * BLOCK + qcol
    mtl = mt_ref[...]

    def attend(nc):
        m_locs, l_locs, pvs = [], [], []
        for c in range(nc):
            r0 = c * BLOCK
            kcc = kc_ref[r0:r0 + BLOCK, :]
            vcc = vct_ref[c]
            valid = (r0 + krow) * CMP_STRIDE + (CMP_BLOCK - 1) <= qpos
            ml, ll, pl_ = [], [], []
            for pair in range(2):
                cols = slice(pair * 2 * BLOCK, (pair + 1) * 2 * BLOCK)
                s = jnp.where(valid, _dot(kcc, qt_pairs[pair]), NEG)
                m_loc = jnp.max(s, axis=0, keepdims=True)
                e = jnp.where(valid, jnp.exp(s - m_loc), 0.0)
                e_s[r0:r0 + BLOCK, cols] = e
                ml.append(m_loc)
                ll.append(jnp.sum(e, axis=0, keepdims=True))
                pl_.append(_dot(vcc, e.astype(BF16)))
            m_locs.append(ml)
            l_locs.append(ll)
            pvs.append(pl_)
        scales = [[None, None] for _ in range(nc)]
        for pair in range(2):
            m_fin = m_locs[0][pair]
            for c in range(1, nc):
                m_fin = jnp.maximum(m_fin, m_locs[c][pair])
            ws = [jnp.exp(m_locs[c][pair] - m_fin) for c in range(nc)]
            den = ws[0] * l_locs[0][pair]
            acc = ws[0] * pvs[0][pair]
            for c in range(1, nc):
                den = den + ws[c] * l_locs[c][pair]
                acc = acc + ws[c] * pvs[c][pair]
            inv_l = 1.0 / jnp.maximum(den, 1e-30)
            o = acc * inv_l
            o_ref[:, pair * LANES:(pair + 1) * LANES] = jnp.concatenate([o[:, 0:BLOCK], o[:, BLOCK:2 * BLOCK]],
                                                                        axis=0).T
            for c in range(nc):
                scales[c][pair] = ws[c] * inv_l
        impsel_s[...] = jnp.zeros(impsel_s.shape, F32)
        for c in range(nc):
            r0 = c * BLOCK
            pn = e_s[r0:r0 + BLOCK, :] * jnp.concatenate(scales[c], axis=1)
            imp = pn[:, 0:BLOCK] + pn[:, BLOCK:2 * BLOCK] + pn[:, 2 * BLOCK:3 * BLOCK] + pn[:, 3 * BLOCK:4 * BLOCK]
            hi = imp.astype(BF16)
            r1 = imp - hi.astype(F32)
            mid = r1.astype(BF16)
            lo = (r1 - mid.astype(F32)).astype(BF16)
            b0 = c * (BLOCK // 4)
            impsel_s[b0:b0 + 64, :] += _dot(mtl, hi) + _dot(mtl, mid) + _dot(mtl, lo)

    n_ch = n_cmp // BLOCK
    if n_ch >= 4:
        need = qi // CMP_STRIDE + 1
        levels = sorted({n_ch // 4, n_ch // 2, 3 * n_ch // 4, n_ch})
        for li, level in enumerate(levels):
            lo = levels[li - 1] if li else 0

            @pl.when((need > lo) & (need <= level))
            def _(level=level):
                attend(level)
    else:
        attend(n_ch)

    imp_sel = impsel_s[0:n_blk, :]
    blk = lax.broadcasted_iota(jnp.int32, (n_blk, BLOCK), 0)
    cur = (qi * BLOCK + lax.broadcasted_iota(jnp.int32, (n_blk, BLOCK), 1)) >> 6
    forced = (blk == 0) | (blk == cur) | (blk == cur - 1)
    causal = blk <= cur
    score = jnp.where(forced, -jnp.inf, jnp.where(causal, imp_sel, NEG))
    blk_f = blk.astype(F32)
    chosen = jnp.where(forced | (cur < SEL_TOPK), 1.0, 0.0)
    for _ in range(SEL_TOPK - 3):
        mx = jnp.max(score, axis=0, keepdims=True)
        first = jnp.min(jnp.where(score == mx, blk_f, 1e9), axis=0, keepdims=True)
        pick = blk_f == first
        chosen = jnp.where(pick, 1.0, chosen)
        score = jnp.where(pick, -jnp.inf, score)
    mq_ref[...] = jnp.where(causal & (chosen > 0.5), 0.0, NEG).astype(BF16)


def _importance_matrix():
    mt = np.zeros((64, BLOCK), np.float32)
    ratio = SEL_BLOCK // CMP_STRIDE
    for o in range(-(CMP_BLOCK // CMP_STRIDE - 1), ratio):
        w = max(0, min(SEL_BLOCK, o * CMP_STRIDE + CMP_BLOCK) - max(0, o * CMP_STRIDE)) / CMP_BLOCK
        for r in range(BLOCK // ratio + 1):
            k = r * ratio + o
            if 0 <= k < BLOCK:
                mt[r, k] = w
    return jnp.asarray(mt, dtype=BF16)


def _cmp_attn(qt, kc, vct, seq):
    n_cmp = seq // CMP_STRIDE
    n_blk = max(LANES, seq // SEL_BLOCK)
    n_ch = n_cmp // BLOCK
    mt = _importance_matrix()
    kern = functools.partial(_cmp_attn_kernel, n_cmp=n_cmp, n_blk=n_blk)
    return pl.pallas_call(
        kern,
        out_shape=(jax.ShapeDtypeStruct((seq, 512), F32),
                   jax.ShapeDtypeStruct((2, seq // BLOCK, n_blk, BLOCK), BF16)),
        grid=(2, seq // BLOCK),
        in_specs=[
            pl.BlockSpec((4 * HEAD_DIM, BLOCK), lambda h, i: (h, i)),
            pl.BlockSpec((n_cmp, LANES), lambda h, i: (0, 0)),
            pl.BlockSpec((None, n_ch, HEAD_DIM, BLOCK), lambda h, i: (h, 0, 0, 0)),
            pl.BlockSpec((64, BLOCK), lambda h, i: (0, 0)),
        ],
        out_specs=(pl.BlockSpec((BLOCK, 256), lambda h, i: (i, h)),
                   pl.BlockSpec((None, None, n_blk, BLOCK), lambda h, i: (h, i, 0, 0))),
        scratch_shapes=[pltpu.VMEM((n_cmp, 4 * BLOCK), F32),
                        pltpu.VMEM((n_blk + 64, BLOCK), F32)],
        compiler_params=_cparams(("arbitrary", "arbitrary")),
        name="cmp_attn_topk",
    )(qt, kc, vct, mt)


def _sel_kernel(qi_ref, kt_ref, qt_ref, mq_ref, k_ref, vt_ref, e_ref, tab_ref, b31_ref, o_ref,
                qaug, m_s, acc_s, s_s, *, n_e):
    kvh = pl.program_id(0)
    step = pl.program_id(1)
    qi = qi_ref[step]
    kt = kt_ref[step]
    q_per_sub = SEL_SUB // SEL_TQ
    win_steps = LANES * SEL_BLOCK // SEL_TK
    ncol = 4 * SEL_TQ

    @pl.when(kt == 0)
    def _():
        m_s[...] = jnp.full(m_s.shape, -3e38, F32)
        acc_s[...] = jnp.zeros(acc_s.shape, F32)

    @pl.when(kt % win_steps == 0)
    def _():
        row_half = lax.broadcasted_iota(jnp.int32, (LANES, SEL_TQ), 0) >> 6
        mq = mq_ref[...]
        for h in range(4):
            qh = qt_ref[h * HEAD_DIM:(h + 1) * HEAD_DIM, :].astype(F32)
            both = jnp.concatenate([qh, qh], axis=0)
            qaug[0:LANES, h * SEL_TQ:(h + 1) * SEL_TQ] = jnp.where(row_half == kvh, both, 0.0).astype(BF16)
            qaug[LANES:2 * LANES, h * SEL_TQ:(h + 1) * SEL_TQ] = mq

    sub_near0 = jnp.maximum(0, (qi - SEL_SLAB_HI + q_per_sub - 1) // q_per_sub)

    def run(with_bias, n_sub):
        pcols = [slice(pair * 2 * SEL_TQ, (pair + 1) * 2 * SEL_TQ) for pair in range(2)]
        qas = [qaug[:, cols] for cols in pcols]
        m_runs = [m_s[:, cols] for cols in pcols]
        if with_bias:
            b31s = [jnp.concatenate([jnp.full((1, SEL_TQ), b31_ref[kvh * 4 + pair * 2 + hh], F32)
                                     for hh in range(2)], axis=1) for pair in range(2)]
            m_runs = [m + jnp.where(sub_near0 >= kt * SEL_NSUB, b, 0.0) for m, b in zip(m_runs, b31s)]
        colmax = [jnp.full((8, 2 * SEL_TQ), -3e38, F32) for _ in range(2)]
        n_chunk = n_sub * SEL_SUB // SEL_CHUNK

        def score_chunk(pair, idx):
            r0 = idx * SEL_CHUNK
            j, c = divmod(idx, SEL_SUB // SEL_CHUNK)
            sub = kt * SEL_NSUB + j
            is_far = sub < sub_near0
            kaug = jnp.concatenate([k_ref[r0:r0 + SEL_CHUNK, :],
                                    e_ref[sub % n_e, c * SEL_CHUNK:(c + 1) * SEL_CHUNK, :]], axis=1)
            s = _dot(kaug, qas[pair])
            if with_bias:
                far_shift = jnp.where(is_far, b31s[pair], 0.0)
                blocks = []
                for hh in range(2):
                    col = []
                    for tt in range(SEL_CHUNK // BLOCK):
                        t = c * (SEL_CHUNK // BLOCK) + tt
                        u = jnp.maximum(qi - q_per_sub * sub - t, SEL_SLAB_LO)
                        uidx = jnp.where(is_far, SEL_NSLAB - 1, u - SEL_SLAB_LO)
                        col.append(s[tt * BLOCK:(tt + 1) * BLOCK, hh * SEL_TQ:(hh + 1) * SEL_TQ]
                                   + tab_ref[pair * 2 + hh, uidx])
                    blocks.append(jnp.concatenate(col, axis=0))
                s = jnp.concatenate(blocks, axis=1) + far_shift
            s_s[r0:r0 + SEL_CHUNK, pcols[pair]] = s
            for g in range(SEL_CHUNK // 8):
                colmax[pair] = jnp.maximum(colmax[pair], s[g * 8:(g + 1) * 8, :])

        for idx in range(n_chunk):
            score_chunk(0, idx)
            score_chunk(1, idx)
        m_news = [jnp.maximum(m, jnp.max(cm, axis=0, keepdims=True)) for m, cm in zip(m_runs, colmax)]
        accs = [jnp.exp(m_runs[pair] - m_news[pair]) * acc_s[:, pcols[pair]] for pair in range(2)]
        for idx in range(n_chunk):
            r0 = idx * SEL_CHUNK
            vt = vt_ref[0:SEL_VROWS, r0:r0 + SEL_CHUNK]
            for pair in range(2):
                pr = jnp.exp(s_s[r0:r0 + SEL_CHUNK, pcols[pair]] - m_news[pair]).astype(BF16)
                accs[pair] = accs[pair] + _dot(vt, pr)
        for pair in range(2):
            acc_s[:, pcols[pair]] = accs[pair]
            m_s[:, pcols[pair]] = m_news[pair]

    all_far = kt * SEL_NSUB + SEL_NSUB - 1 < sub_near0
    n_live = jnp.minimum(SEL_NSUB, qi // q_per_sub - kt * SEL_NSUB + 1)

    @pl.when(all_far)
    def _():
        run(False, SEL_NSUB)

    for live in range(1, SEL_NSUB + 1):
        @pl.when(jnp.logical_not(all_far) & (n_live == live))
        def _(live=live):
            run(True, live)

    @pl.when(kt == qi // (SEL_TK // SEL_TQ))
    def _():
        acc = acc_s[...]
        o = acc[0:HEAD_DIM, :] / acc[HEAD_DIM:HEAD_DIM + 1, :]
        for jp in range(2):
            blk = jnp.concatenate([o[:, (2 * jp) * SEL_TQ:(2 * jp + 1) * SEL_TQ],
                                   o[:, (2 * jp + 1) * SEL_TQ:(2 * jp + 2) * SEL_TQ]], axis=0)
            o_ref[:, jp * LANES:(jp + 1) * LANES] = blk.T


def _sel_attn(qkv, qt, vt, mq, bias_s, rel_bias, seq):
    n_q = seq // SEL_TQ
    per_step = SEL_TK // SEL_TQ
    qi_l, kt_l = [], []
    for qi in range(n_q):
        for kt in range(qi // per_step + 1):
            qi_l.append(qi)
            kt_l.append(kt)
    qi_arr = jnp.asarray(np.array(qi_l, np.int32))
    kt_arr = jnp.asarray(np.array(kt_l, np.int32))
    win_steps = LANES * SEL_BLOCK // SEL_TK
    n_e = min(LANES * SEL_BLOCK // SEL_SUB, seq // SEL_SUB)
    e_np = np.zeros((n_e, SEL_SUB, LANES), np.float32)
    for t in range(n_e):
        for k in range(SEL_SUB):
            e_np[t, k, (t * (SEL_SUB // SEL_BLOCK) + k // SEL_BLOCK) % LANES] = 1.0
    e_all = jnp.asarray(e_np, dtype=BF16)
    b31 = rel_bias[REL_BUCKETS - 1, 4:12]
    grid_spec = pltpu.PrefetchScalarGridSpec(
        num_scalar_prefetch=2,
        grid=(2, len(qi_l)),
        in_specs=[
            pl.BlockSpec((4 * HEAD_DIM, SEL_TQ), lambda h, s, qi, kt: (h, qi[s])),
            pl.BlockSpec((None, None, LANES, SEL_TQ), lambda h, s, qi, kt: (h, qi[s], kt[s] // win_steps, 0)),
            pl.BlockSpec((SEL_TK, LANES), lambda h, s, qi, kt: (kt[s], _qkv_col(COL_CKS) // LANES)),
            pl.BlockSpec((None, LANES, SEL_TK), lambda h, s, qi, kt: (h, 0, kt[s])),
            pl.BlockSpec((n_e, SEL_SUB, LANES), lambda h, s, qi, kt: (0, 0, 0)),
            pl.BlockSpec((4, SEL_NSLAB, BLOCK, BLOCK), lambda h, s, qi, kt: (h, 0, 0, 0)),
            pl.BlockSpec(memory_space=pltpu.SMEM),
        ],
        out_specs=pl.BlockSpec((SEL_TQ, 256), lambda h, s, qi, kt: (qi[s], h)),
        scratch_shapes=[pltpu.VMEM((2 * LANES, 4 * SEL_TQ), BF16),
                        pltpu.VMEM((1, 4 * SEL_TQ), F32),
                        pltpu.VMEM((SEL_VROWS, 4 * SEL_TQ), F32),
                        pltpu.VMEM((SEL_TK, 4 * SEL_TQ), F32)],
    )
    return pl.pallas_call(
        functools.partial(_sel_kernel, n_e=n_e),
        out_shape=jax.ShapeDtypeStruct((seq, 512), F32),
        grid_spec=grid_spec,
        compiler_params=_cparams(("arbitrary", "arbitrary")),
        name="selected_attn",
    )(qi_arr, kt_arr, qt, mq, qkv, vt, e_all, bias_s, b31)


def _layer_norm(z, g, b):
    mu = jnp.mean(z, axis=-1, keepdims=True)
    zc = z - mu
    var = jnp.mean(jnp.square(zc), axis=-1, keepdims=True)
    return zc * lax.rsqrt(var + LN_EPS) * g + b


def _outproj_kernel(x_ref, oa0, oa1, oa2, la0, la1, la2, ob_ref, ocmp_ref, oslc_ref, owin_ref, g_ref,
                    gx_ref, w_ref, ga_ref, lng_ref, lnb_ref, o_ref):
    l0, l1, l2 = la0[...], la1[...], la2[...]
    mx = jnp.maximum(jnp.maximum(l0, l1), l2)
    w0, w1, w2 = jnp.exp(l0 - mx), jnp.exp(l1 - mx), jnp.exp(l2 - mx)
    oa = (w0 * oa0[...] + w1 * oa1[...] + w2 * oa2[...]) / (w0 + w1 + w2)
    gt = jax.nn.sigmoid(g_ref[...])
    hi = gt.astype(BF16)
    lo = (gt - hi.astype(F32)).astype(BF16)
    gx = gx_ref[...]
    gates = _dot(hi, gx) + _dot(lo, gx)
    oc = gates[:, 0:512] * ocmp_ref[...] + gates[:, 512:1024] * oslc_ref[...] + gates[:, 1024:1536] * owin_ref[...]
    mixed = jnp.concatenate([oa, ob_ref[...], oc], axis=1).astype(BF16)
    y = _dot(mixed, w_ref[...])
    z = DN_ALPHA * x_ref[...] + (1 + ga_ref[...]) * y
    o_ref[...] = _layer_norm(z, lng_ref[...], lnb_ref[...])


def _gate_expand():
    gx = np.zeros((LANES, 3 * 512), np.float32)
    for h in range(8):
        for b in range(3):
            gx[h * 3 + b, b * 512 + h * HEAD_DIM:b * 512 + (h + 1) * HEAD_DIM] = 1.0
    return jnp.asarray(gx, dtype=BF16)


def _outproj(x, oa, la, ob, ocmp, oslc, owin, gates, w_out_l, mod_l, lng, lnb, seq):
    tm = 256
    row = lambda w: pl.BlockSpec((tm, w), lambda i: (i, 0))
    vec = lambda k: pl.BlockSpec((1, D_MODEL), lambda i, k=k: (0, k))
    return pl.pallas_call(
        _outproj_kernel,
        out_shape=jax.ShapeDtypeStruct((seq, D_MODEL), F32),
        grid=(seq // tm,),
        in_specs=[row(D_MODEL)] + [row(256)] * 7 + [row(512)] * 3 + [row(LANES)] + [
            pl.BlockSpec((LANES, 3 * 512), lambda i: (0, 0)),
            pl.BlockSpec((D_MODEL, D_MODEL), lambda i: (0, 0)),
            vec(2), vec(0), vec(0)],
        out_specs=row(D_MODEL),
        compiler_params=_cparams(("arbitrary",)),
        name="out_proj_ln",
    )(x, oa[0], oa[1], oa[2], la[0], la[1], la[2], ob, ocmp, oslc, owin, gates, _gate_expand(), w_out_l,
      mod_l, lng, lnb)


def _mlp_kernel(x_ref, sc_ref, sh_ref, g_ref, w1_ref, w2_ref, lng_ref, lnb_ref, o_ref, h_s, acc_s):
    j = pl.program_id(1)

    @pl.when(j == 0)
    def _():
        h_s[...] = (x_ref[...] * (1 + sc_ref[...]) + sh_ref[...]).astype(BF16)
        acc_s[...] = jnp.zeros(acc_s.shape, F32)

    f = jnp.maximum(_dot(h_s[...], w1_ref[...]), 0.0)
    acc_s[...] += _dot(jnp.square(f).astype(BF16), w2_ref[...])

    @pl.when(j == pl.num_programs(1) - 1)
    def _():
        z = DN_ALPHA * x_ref[...] + (1 + g_ref[...]) * acc_s[...]
        o_ref[...] = _layer_norm(z, lng_ref[...], lnb_ref[...])


def _mlp(x, mod_l, w1_l, w2_l, lng, lnb, seq):
    tm = 1024
    tf = 1024
    vec = lambda k: pl.BlockSpec((1, D_MODEL), lambda i, j, k=k: (0, k))
    return pl.pallas_call(
        _mlp_kernel,
        out_shape=jax.ShapeDtypeStruct((seq, D_MODEL), F32),
        grid=(seq // tm, D_FF // tf),
        in_specs=[
            pl.BlockSpec((tm, D_MODEL), lambda i, j: (i, 0)),
            vec(4), vec(3), vec(5),
            pl.BlockSpec((D_MODEL, tf), lambda i, j: (0, j)),
            pl.BlockSpec((tf, D_MODEL), lambda i, j: (j, 0)),
            vec(0), vec(0),
        ],
        out_specs=pl.BlockSpec((tm, D_MODEL), lambda i, j: (i, 0)),
        scratch_shapes=[pltpu.VMEM((tm, D_MODEL), BF16), pltpu.VMEM((tm, D_MODEL), F32)],
        compiler_params=_cparams(("arbitrary", "arbitrary")),
        name="mlp_ln",
    )(x, mod_l, mod_l, mod_l, w1_l, w2_l, lng, lnb)


def kernel(x, c, positions, w_in, w_out, rel_bias, sinks, cmp_pos, cmp_w1, cmp_w2, mlp_w1, mlp_w2,
           ada_w, ada_b, ln_g, ln_b):
    bsz, seq, d = x.shape
    assert bsz == 1 and d == D_MODEL and seq % (2 * A_CONFIGS[-1][0]) == 0
    x = x.reshape(seq, D_MODEL)

    mod = _modulation(c, ada_w, ada_b)
    cos_t, sin_t = _rope_tables(positions, seq)
    bias_a, bias_w, bias_s = _bias_tables(rel_bias)

    w_in_b = jnp.pad(w_in, ((0, 0), (0, 0), (0, N_IN_PAD - N_IN))).astype(BF16)
    w_out_b = w_out.astype(BF16)
    w1_b = mlp_w1.astype(BF16)
    w2_b = mlp_w2.astype(BF16)
    cw1_b = cmp_w1.astype(BF16)
    cw2_b = cmp_w2.astype(BF16)

    for l in range(DEPTH):
        mod_l = mod[l]
        qkv, ck, gates, qt, vt, a_qkv = _inproj(x, mod_l, w_in_b[l], cos_t, sin_t, seq)

        oa, la = [], []
        for ci, (window, dil) in enumerate(A_CONFIGS):
            nrows = seq // dil
            view = a_qkv.reshape(nrows, dil * N_A)
            o_c, l_c = _banded_call(
                view, nrows, dil, N_A // LANES, 0, 2, 4, 2,
                nprev=1, tq=min(BAND_TQ, nrows), max_dist=window // dil, bias=bias_a[ci], want_lse=True,
                group_div=None, kv_per_qblock=True, name=f"dilated_attn_{dil}")
            oa.append(o_c.reshape(seq, 256))
            la.append(l_c.reshape(seq, 256))

        (ob,) = _banded_call(
            qkv, seq, 1, N_QKV // LANES, _qkv_col(COL_BQ) // LANES, _qkv_col(COL_BK) // LANES,
            _qkv_col(COL_BV) // LANES, 2,
            nprev=1, tq=BAND_TQ, max_dist=B_WINDOW - 1, sinks=sinks[l], group_div=1,
            kv_per_qblock=False, name="swa_attn")

        kc, vct = _compress(ck, cmp_pos[l], cw1_b[l], cw2_b[l], seq)
        ocmp, mq = _cmp_attn(qt, kc, vct, seq)
        oslc = _sel_attn(qkv, qt, vt, mq, bias_s, rel_bias, seq)
        (owin,) = _banded_call(
            qkv, seq, 1, N_QKV // LANES, _qkv_col(COL_CQ) // LANES, _qkv_col(COL_CKW) // LANES,
            _qkv_col(COL_CVW) // LANES, 4,
            nprev=C_WINDOW // BLOCK, tq=BAND_TQ, max_dist=C_WINDOW - 1, bias=bias_w, group_div=2,
            kv_per_qblock=False, name="window_attn")

        x = _outproj(x, oa, la, ob, ocmp, oslc, owin, gates, w_out_b[l], mod_l,
                     ln_g[l, 0:1], ln_b[l, 0:1], seq)
        x = _mlp(x, mod_l, w1_b[l], w2_b[l], ln_g[l, 1:2], ln_b[l, 1:2], seq)

    return x.reshape(bsz, seq, D_MODEL)
```

```python
import functools
import math

import numpy as np
import jax
import jax.numpy as jnp
from jax import lax
from jax.experimental import pallas as pl
from jax.experimental.pallas import tpu as pltpu

F32 = jnp.float32
BF16 = jnp.bfloat16

D_MODEL = 1024
DEPTH = 4
HEAD_DIM = 64
LANES = 128
BLOCK = 128
A_CONFIGS = ((128, 1), (512, 4), (2048, 16))
B_WINDOW = 128
ROPE_THETA = 150000.0
CMP_BLOCK = 32
CMP_STRIDE = 16
CMP_HIDDEN = 256
SEL_BLOCK = 64
SEL_TOPK = 16
C_WINDOW = 512
REL_BUCKETS = 32
REL_MAX_DIST = 2048
D_FF = 4 * D_MODEL
DN_ALPHA = (2 * DEPTH) ** 0.25
LN_EPS = 1e-5
NEG = -1e30
FORCE = 1e4
SCALE = HEAD_DIM ** -0.5

COL_AQ, COL_AK, COL_AV = 0, 256, 512
COL_BQ, COL_BK, COL_BV = 768, 1024, 1152
COL_CQ = 1280
COL_CKC = 1792
COL_CKS, COL_CVS = 2048, 2176
COL_CKW, COL_CVW = 2304, 2432
COL_CG = 2560
N_A = 768
N_QKV = 1536


def _qkv_col(col):
    assert COL_BQ <= col < COL_CKC or COL_CKS <= col < COL_CG
    return col - N_A if col < COL_CKC else col - N_A - 256
N_IN = 2584
N_IN_PAD = 2688

BAND_TQ = 1024
SEL_TQ = 128
SEL_SUB = 512
SEL_CHUNK = 128
SEL_VROWS = 80
SEL_NSUB = 4
SEL_TK = SEL_SUB * SEL_NSUB
SEL_SLAB_LO = -1
SEL_SLAB_HI = 15
SEL_NSLAB = SEL_SLAB_HI - SEL_SLAB_LO + 2
VMEM_LIMIT = 56 * 1024 * 1024


def _cparams(sem, flags=None):
    return pltpu.CompilerParams(dimension_semantics=sem, vmem_limit_bytes=VMEM_LIMIT, flags=flags)


def _dot_nt(a, b):
    return lax.dot_general(a, b, (((1,), (1,)), ((), ())), preferred_element_type=F32)


def _dot(a, b):
    return jnp.dot(a, b, preferred_element_type=F32)


def _mod_kernel(c_ref, w_ref, b_ref, o_ref):
    c = c_ref[...]
    act = c * jax.nn.sigmoid(c)
    lhs = jnp.broadcast_to(act, (8, D_MODEL)).astype(BF16)
    y = _dot(lhs, w_ref[...].astype(BF16))
    o_ref[...] = y[0:1, :] + b_ref[...]


def _modulation(c, ada_w, ada_b):
    tn = 1536
    return pl.pallas_call(
        _mod_kernel,
        out_shape=jax.ShapeDtypeStruct((DEPTH, 1, 6 * D_MODEL), F32),
        grid=(DEPTH, 6 * D_MODEL // tn),
        in_specs=[
            pl.BlockSpec((1, D_MODEL), lambda l, j: (0, 0)),
            pl.BlockSpec((None, D_MODEL, tn), lambda l, j: (l, 0, j)),
            pl.BlockSpec((None, 1, tn), lambda l, j: (l, 0, j)),
        ],
        out_specs=pl.BlockSpec((None, 1, tn), lambda l, j: (l, 0, j)),
        compiler_params=_cparams(("arbitrary", "arbitrary")),
        name="adaln_mod",
    )(c, ada_w, ada_b.reshape(DEPTH, 1, 6 * D_MODEL))


def _rope_table_kernel(pos_ref, freq_ref, cos_ref, sin_ref):
    ang = pos_ref[...].astype(F32) * freq_ref[...]
    cos_ref[...] = jnp.cos(ang)
    sin_ref[...] = jnp.sin(ang)


def _rope_tables(positions, seq):
    half = HEAD_DIM // 2
    freq = ROPE_THETA ** (-jnp.arange(half, dtype=F32) / half)
    freq = jnp.tile(freq, LANES // half).reshape(1, LANES)
    tm = 1024
    return pl.pallas_call(
        _rope_table_kernel,
        out_shape=(jax.ShapeDtypeStruct((seq, LANES), F32),) * 2,
        grid=(seq // tm,),
        in_specs=[pl.BlockSpec((tm, 1), lambda i: (i, 0)), pl.BlockSpec((1, LANES), lambda i: (0, 0))],
        out_specs=(pl.BlockSpec((tm, LANES), lambda i: (i, 0)),) * 2,
        compiler_params=_cparams(("arbitrary",)),
        name="rope_tables",
    )(positions.reshape(seq, 1), freq)


def _t5_bias(rb_ref, dist, col, max_dist):
    exact = REL_BUCKETS // 2
    d = jnp.maximum(dist, 0)
    df = jnp.maximum(d, 1).astype(F32)
    large = exact + (jnp.log(df / exact) / math.log(REL_MAX_DIST / exact)
                     * (REL_BUCKETS - exact)).astype(jnp.int32)
    large = jnp.minimum(large, REL_BUCKETS - 1)
    bucket = jnp.where(d < exact, d, large)
    out = jnp.zeros(dist.shape, F32)
    for b in range(REL_BUCKETS):
        out = jnp.where(bucket == b, rb_ref[b * 12 + col], out)
    return out, bucket


def _bias_a_kernel(rb_ref, o_ref):
    cfg = pl.program_id(0)
    head = pl.program_id(1)
    dil = jnp.where(cfg == 0, 1, jnp.where(cfg == 1, 4, 16))
    r = lax.broadcasted_iota(jnp.int32, (BLOCK, 2 * BLOCK), 0)
    c = lax.broadcasted_iota(jnp.int32, (BLOCK, 2 * BLOCK), 1)
    dist = r + BLOCK - c
    bias, _ = _t5_bias(rb_ref, dist * dil, head, None)
    o_ref[...] = jnp.where((dist >= 0) & (dist <= BLOCK), bias, NEG)


def _bias_w_kernel(rb_ref, o_ref):
    head = pl.program_id(0)
    nprev = C_WINDOW // BLOCK
    r = lax.broadcasted_iota(jnp.int32, (BLOCK, (nprev + 1) * BLOCK), 0)
    c = lax.broadcasted_iota(jnp.int32, (BLOCK, (nprev + 1) * BLOCK), 1)
    dist = r + nprev * BLOCK - c
    bias, _ = _t5_bias(rb_ref, dist, 4 + head, None)
    o_ref[...] = jnp.where((dist >= 0) & (dist <= C_WINDOW - 1), bias, NEG)


def _bias_s_kernel(rb_ref, o_ref):
    head = pl.program_id(0)
    slab = pl.program_id(1)
    r = lax.broadcasted_iota(jnp.int32, (BLOCK, BLOCK), 1)
    c = lax.broadcasted_iota(jnp.int32, (BLOCK, BLOCK), 0)
    dist = (slab + SEL_SLAB_LO) * BLOCK + r - c
    bias, _ = _t5_bias(rb_ref, dist, 4 + head, None)
    val = jnp.where(dist >= 0, bias, NEG)
    o_ref[...] = jnp.where(slab == SEL_NSLAB - 1, 0.0, val)


def _bias_tables(rel_bias):
    rb = rel_bias.reshape(-1)
    smem = pl.BlockSpec(memory_space=pltpu.SMEM)
    bias_a = pl.pallas_call(
        _bias_a_kernel,
        out_shape=jax.ShapeDtypeStruct((3, 4, BLOCK, 2 * BLOCK), F32),
        grid=(3, 4),
        in_specs=[smem],
        out_specs=pl.BlockSpec((None, None, BLOCK, 2 * BLOCK), lambda a, h: (a, h, 0, 0)),
        compiler_params=_cparams(("arbitrary", "arbitrary")),
        name="bias_table_dilated",
    )(rb)
    wctx = C_WINDOW + BLOCK
    bias_w = pl.pallas_call(
        _bias_w_kernel,
        out_shape=jax.ShapeDtypeStruct((8, BLOCK, wctx), F32),
        grid=(8,),
        in_specs=[smem],
        out_specs=pl.BlockSpec((None, BLOCK, wctx), lambda h: (h, 0, 0)),
        compiler_params=_cparams(("arbitrary",)),
        name="bias_table_window",
    )(rb)
    bias_s = pl.pallas_call(
        _bias_s_kernel,
        out_shape=jax.ShapeDtypeStruct((8, SEL_NSLAB, BLOCK, BLOCK), F32),
        grid=(8, SEL_NSLAB),
        in_specs=[smem],
        out_specs=pl.BlockSpec((None, None, BLOCK, BLOCK), lambda h, u: (h, u, 0, 0)),
        compiler_params=_cparams(("arbitrary", "arbitrary")),
        name="bias_table_selected",
    )(rb)
    return bias_a, bias_w, bias_s


def _rope_apply(t, cos, sin, first):
    up = pltpu.roll(t, LANES - HEAD_DIM // 2, axis=1)
    dn = pltpu.roll(t, HEAD_DIM // 2, axis=1)
    return t * cos + jnp.where(first, -up, dn) * sin


def _inproj_kernel(x_ref, sc_ref, sh_ref, w_ref, cos_ref, sin_ref, qkv_ref, ck_ref, g_ref, qt_ref, vt_ref,
                   a_ref):
    h = (x_ref[...] * (1 + sc_ref[...]) + sh_ref[...]).astype(BF16)
    y = _dot(h, w_ref[...])
    cos = cos_ref[...]
    sin = sin_ref[...]
    lane = lax.broadcasted_iota(jnp.int32, cos.shape, 1)
    first = (lane & (HEAD_DIM - 1)) < HEAD_DIM // 2

    def put(c0, width, val):
        q0 = _qkv_col(c0)
        qkv_ref[:, q0:q0 + width] = val.astype(BF16)

    a_ref[:, 0:256] = (y[:, COL_AQ:COL_AQ + 256] * SCALE).astype(BF16)
    a_ref[:, 256:768] = y[:, COL_AK:COL_AK + 512].astype(BF16)
    for j in range(2):
        c0 = COL_BQ + j * LANES
        put(c0, LANES, _rope_apply(y[:, c0:c0 + LANES], cos, sin, first) * SCALE)
    put(COL_BK, LANES, _rope_apply(y[:, COL_BK:COL_BK + LANES], cos, sin, first))
    put(COL_BV, LANES, y[:, COL_BV:COL_BV + LANES])
    put(COL_CQ, 512, y[:, COL_CQ:COL_CQ + 512] * SCALE)
    put(COL_CKS, 512, y[:, COL_CKS:COL_CKS + 512])
    for g in range(4):
        c0 = COL_CKC + g * HEAD_DIM
        ck_ref[g] = y[:, c0:c0 + HEAD_DIM]
    g_ref[...] = y[:, COL_CG:COL_CG + LANES]
    qt_ref[...] = (y[:, COL_CQ:COL_CQ + 512] * SCALE).T.astype(BF16)
    vt = y[:, COL_CVS:COL_CVS + LANES].T
    ones_row = (lax.broadcasted_iota(jnp.int32, (HEAD_DIM, vt.shape[1]), 0) == 0).astype(F32)
    for kvh in range(2):
        vt_ref[kvh] = jnp.concatenate([vt[kvh * HEAD_DIM:(kvh + 1) * HEAD_DIM, :], ones_row], axis=0).astype(BF16)


def _inproj(x, mod_l, w_in_l, cos_t, sin_t, seq):
    tm = 512
    vec = lambda k: pl.BlockSpec((1, D_MODEL), lambda i, k=k: (0, k))
    return pl.pallas_call(
        _inproj_kernel,
        out_shape=(jax.ShapeDtypeStruct((seq, N_QKV), BF16),
                   jax.ShapeDtypeStruct((4, seq, HEAD_DIM), F32),
                   jax.ShapeDtypeStruct((seq, LANES), F32),
                   jax.ShapeDtypeStruct((512, seq), BF16),
                   jax.ShapeDtypeStruct((2, LANES, seq), BF16),
                   jax.ShapeDtypeStruct((seq, N_A), BF16)),
        grid=(seq // tm,),
        in_specs=[
            pl.BlockSpec((tm, D_MODEL), lambda i: (i, 0)),
            vec(1), vec(0),
            pl.BlockSpec((D_MODEL, N_IN_PAD), lambda i: (0, 0)),
            pl.BlockSpec((tm, LANES), lambda i: (i, 0)),
            pl.BlockSpec((tm, LANES), lambda i: (i, 0)),
        ],
        out_specs=(pl.BlockSpec((tm, N_QKV), lambda i: (i, 0)),
                   pl.BlockSpec((4, tm, HEAD_DIM), lambda i: (0, i, 0)),
                   pl.BlockSpec((tm, LANES), lambda i: (i, 0)),
                   pl.BlockSpec((512, tm), lambda i: (0, i)),
                   pl.BlockSpec((2, LANES, tm), lambda i: (0, 0, i)),
                   pl.BlockSpec((tm, N_A), lambda i: (i, 0))),
        compiler_params=_cparams(("arbitrary",)),
        name="in_proj",
    )(x, mod_l, mod_l, w_in_l, cos_t, sin_t)


def _lane_half():
    return lax.broadcasted_iota(jnp.int32, (BLOCK, LANES), 1) >> 6


def _align_q(q, half, x, ysel):
    qx = jnp.where(half == x, q, 0.0)
    if ysel is None:
        return qx.astype(BF16)
    qd = qx + pltpu.roll(qx, HEAD_DIM, axis=1)
    return jnp.where(ysel, qd, 0.0).astype(BF16)


def _spread_o(o, ysel):
    if ysel is None:
        return o
    ob = jnp.where(ysel, o, 0.0)
    return ob + pltpu.roll(ob, HEAD_DIM, axis=1)


def _banded_kernel(*refs, nprev, tq, max_dist, use_bias, use_sinks, want_lse, group_div, row_axis):
    refs = list(refs)
    sink_ref = refs.pop(0) if use_sinks else None
    q_ref, kp_ref, kc_ref, vp_ref, vc_ref = refs[:5]
    refs = refs[5:]
    bias_ref = refs.pop(0) if use_bias else None
    o_ref = refs.pop(0)
    lse_ref = refs.pop(0) if want_lse else None
    kctx, vctx = refs

    tp = nprev * BLOCK
    ctx = (nprev + 1) * BLOCK
    i = pl.program_id(row_axis)
    p = pl.program_id(row_axis - 1)
    kctx[0:tp, :] = kp_ref[...]
    kctx[tp:tp + tq, :] = kc_ref[...]
    vctx[0:tp, :] = vp_ref[...]
    vctx[tp:tp + tq, :] = vc_ref[...]

    half = _lane_half()
    upper = half == 1
    ysel = None if group_div is None else half == p // group_div
    chunks = [(c0, min(c0 + 2 * BLOCK, ctx)) for c0 in range(0, ctx, 2 * BLOCK)]
    rrs, ccs, bands = [], [], []
    for c0, c1 in chunks:
        rr = lax.broadcasted_iota(jnp.int32, (BLOCK, c1 - c0), 0)
        cc = lax.broadcasted_iota(jnp.int32, (BLOCK, c1 - c0), 1) + c0
        ccs.append(cc)
        if not use_bias:
            dist = rr + tp - cc
            bands.append((dist >= 0) & (dist <= max_dist))

    for sub in range(tq // BLOCK):
        rows = slice(sub * BLOCK, (sub + 1) * BLOCK)
        q = q_ref[rows, :].astype(F32)
        low = tp - (i * tq + sub * BLOCK)
        outs, lses = [], []
        q2 = jnp.concatenate([_align_q(q, half, x, ysel) for x in range(2)], axis=0)
        stats = [([], [], []), ([], [], [])]
        for ci, (c0, c1) in enumerate(chunks):
            k = kctx[sub * BLOCK + c0:sub * BLOCK + c1, :]
            v = vctx[sub * BLOCK + c0:sub * BLOCK + c1, :]
            s2 = _dot_nt(q2, k)
            es = []
            for x in range(2):
                s = s2[x * BLOCK:(x + 1) * BLOCK, :]
                if use_bias:
                    s = s + bias_ref[x, :, c0:c1]
                else:
                    s = jnp.where(bands[ci], s, NEG)
                s = jnp.where(ccs[ci] >= low, s, NEG)
                m_loc = jnp.max(s, axis=1, keepdims=True)
                e = jnp.exp(s - m_loc)
                stats[x][0].append(m_loc)
                stats[x][1].append(jnp.sum(e, axis=1, keepdims=True))
                es.append(e.astype(BF16))
            pv2 = _dot(jnp.concatenate(es, axis=0), v)
            for x in range(2):
                stats[x][2].append(pv2[x * BLOCK:(x + 1) * BLOCK, :])
        for x in range(2):
            ms, dens, pvs = stats[x]
            m = ms[0]
            for m_loc in ms[1:]:
                m = jnp.maximum(m, m_loc)
            if use_sinks:
                sk = sink_ref[2 * p + x]
                m = jnp.maximum(m, sk)
            ws = [jnp.exp(m_loc - m) for m_loc in ms]
            den = ws[0] * dens[0]
            acc = ws[0] * pvs[0]
            for w, d, pv in zip(ws[1:], dens[1:], pvs[1:]):
                den = den + w * d
                acc = acc + w * pv
            if use_sinks:
                den = den + jnp.exp(sk - m)
            o = acc / den
            outs.append(_spread_o(o, ysel))
            lses.append(m + jnp.log(den))
        o_ref[rows, :] = jnp.where(upper, outs[1], outs[0])
        if want_lse:
            lse_ref[rows, :] = jnp.where(upper, jnp.broadcast_to(lses[1], (BLOCK, LANES)),
                                         jnp.broadcast_to(lses[0], (BLOCK, LANES)))


def _banded_call(qkv_view, nrows, nsub, row_width_blocks, q_cb, k_cb, v_cb, n_qblocks, *, nprev, tq,
                 max_dist, bias=None, sinks=None, want_lse=False, group_div=None, kv_per_qblock, name):
    tp = nprev * BLOCK
    ratio = tq // tp
    ctx = (nprev + 1) * BLOCK

    def kvb(p):
        return p if kv_per_qblock else 0

    in_specs = []
    args = []
    if sinks is not None:
        in_specs.append(pl.BlockSpec(memory_space=pltpu.SMEM))
        args.append(sinks)
    in_specs += [
        pl.BlockSpec((tq, LANES), lambda r, p, i: (i, r * row_width_blocks + q_cb + p)),
        pl.BlockSpec((tp, LANES), lambda r, p, i: (jnp.maximum(i * ratio - 1, 0), r * row_width_blocks + k_cb + kvb(p))),
        pl.BlockSpec((tq, LANES), lambda r, p, i: (i, r * row_width_blocks + k_cb + kvb(p))),
        pl.BlockSpec((tp, LANES), lambda r, p, i: (jnp.maximum(i * ratio - 1, 0), r * row_width_blocks + v_cb + kvb(p))),
        pl.BlockSpec((tq, LANES), lambda r, p, i: (i, r * row_width_blocks + v_cb + kvb(p))),
    ]
    args += [qkv_view] * 5
    if bias is not None:
        in_specs.append(pl.BlockSpec((2, BLOCK, ctx), lambda r, p, i: (p, 0, 0)))
        args.append(bias)
    out_w = nsub * n_qblocks * LANES
    o_spec = pl.BlockSpec((tq, LANES), lambda r, p, i: (i, r * n_qblocks + p))
    out_shape = [jax.ShapeDtypeStruct((nrows, out_w), F32)]
    out_specs = [o_spec]
    if want_lse:
        out_shape.append(jax.ShapeDtypeStruct((nrows, out_w), F32))
        out_specs.append(o_spec)
    kern = functools.partial(_banded_kernel, nprev=nprev, tq=tq, max_dist=max_dist, use_bias=bias is not None,
                             use_sinks=sinks is not None, want_lse=want_lse, group_div=group_div, row_axis=2)
    return pl.pallas_call(
        kern,
        out_shape=tuple(out_shape),
        grid=(nsub, n_qblocks, nrows // tq),
        in_specs=in_specs,
        out_specs=tuple(out_specs),
        scratch_shapes=[pltpu.VMEM((tp + tq, LANES), BF16), pltpu.VMEM((tp + tq, LANES), BF16)],
        compiler_params=_cparams(("arbitrary", "arbitrary", "arbitrary")),
        name=name,
    )(*args)


def _compress_kernel(x_ref, pos_ref, w1_ref, w2_ref, o_ref):
    x = x_ref[...]
    half = CMP_STRIDE * HEAD_DIM
    xa = (x + pos_ref[0:1, :]).astype(BF16)
    xb = (x + pos_ref[1:2, :]).astype(BF16)
    first = _dot(xa, w1_ref[0:half, :])
    second = _dot(xb, w1_ref[half:2 * half, :])
    n = x.shape[0]
    hid = first + pltpu.roll(second, n - 1, axis=0)
    act = jax.nn.gelu(hid)
    o_ref[...] = _dot(act.astype(BF16), w2_ref[...]).astype(BF16)


def _compress(ck, cmp_pos_l, w1_l, w2_l, seq):
    n_chunk = seq // CMP_STRIDE
    feat = CMP_STRIDE * HEAD_DIM
    x = ck.reshape(4, n_chunk, feat)
    pos = cmp_pos_l.reshape(2, 2, feat)
    out = pl.pallas_call(
        _compress_kernel,
        out_shape=jax.ShapeDtypeStruct((4, n_chunk, HEAD_DIM), BF16),
        grid=(4,),
        in_specs=[
            pl.BlockSpec((None, n_chunk, feat), lambda g: (g, 0, 0)),
            pl.BlockSpec((None, 2, feat), lambda g: (g // 2, 0, 0)),
            pl.BlockSpec((None, 2 * feat, CMP_HIDDEN), lambda g: (g // 2, 0, 0)),
            pl.BlockSpec((None, CMP_HIDDEN, HEAD_DIM), lambda g: (g // 2, 0, 0)),
        ],
        out_specs=pl.BlockSpec((None, n_chunk, HEAD_DIM), lambda g: (g, 0, 0)),
        compiler_params=_cparams(("arbitrary",)),
        name="compress_tokens",
    )(x, pos, w1_l, w2_l)
    kc = jnp.concatenate([out[0], out[1]], axis=1)
    vct = out[2:4].reshape(2, n_chunk // BLOCK, BLOCK, HEAD_DIM).transpose(0, 1, 3, 2)
    return kc, vct


def _cmp_attn_kernel(qt_ref, kc_ref, vct_ref, mt_ref, o_ref, mq_ref, e_s, impsel_s, *, n_cmp, n_blk):
    kvh = pl.program_id(0)
    qi = pl.program_id(1)
    row_half = lax.broadcasted_iota(jnp.int32, (LANES, BLOCK), 0) >> 6
    qts = []
    for h in range(4):
        qh = qt_ref[h * HEAD_DIM:(h + 1) * HEAD_DIM, :].astype(F32)
        both = jnp.concatenate([qh, qh], axis=0)
        qts.append(jnp.where(row_half == kvh, both, 0.0).astype(BF16))
    qt_pairs = [jnp.concatenate(qts[0:2], axis=1), jnp.concatenate(qts[2:4], axis=1)]

    krow = lax.broadcasted_iota(jnp.int32, (BLOCK, 2 * BLOCK), 0)
    qcol = lax.broadcasted_iota(jnp.int32, (BLOCK, 2 * BLOCK), 1) & (BLOCK - 1)
    qpos = qi * BLOCK + qcol
    mtl = mt_ref[...]

    def attend(nc):
        m_locs, l_locs, pvs = [], [], []
        for c in range(nc):
            r0 = c * BLOCK
            kcc = kc_ref[r0:r0 + BLOCK, :]
            vcc = vct_ref[c]
            valid = (r0 + krow) * CMP_STRIDE + (CMP_BLOCK - 1) <= qpos
            ml, ll, pl_ = [], [], []
            for pair in range(2):
                cols = slice(pair * 2 * BLOCK, (pair + 1) * 2 * BLOCK)
                s = jnp.where(valid, _dot(kcc, qt_pairs[pair]), NEG)
                m_loc = jnp.max(s, axis=0, keepdims=True)
                e = jnp.where(valid, jnp.exp(s - m_loc), 0.0)
                e_s[r0:r0 + BLOCK, cols] = e
                ml.append(m_loc)
                ll.append(jnp.sum(e, axis=0, keepdims=True))
                pl_.append(_dot(vcc, e.astype(BF16)))
            m_locs.append(ml)
            l_locs.append(ll)
            pvs.append(pl_)
        scales = [[None, None] for _ in range(nc)]
        for pair in range(2):
            m_fin = m_locs[0][pair]
            for c in range(1, nc):
                m_fin = jnp.maximum(m_fin, m_locs[c][pair])
            ws = [jnp.exp(m_locs[c][pair] - m_fin) for c in range(nc)]
            den = ws[0] * l_locs[0][pair]
            acc = ws[0] * pvs[0][pair]
            for c in range(1, nc):
                den = den + ws[c] * l_locs[c][pair]
                acc = acc + ws[c] * pvs[c][pair]
            inv_l = 1.0 / jnp.maximum(den, 1e-30)
            o = acc * inv_l
            o_ref[:, pair * LANES:(pair + 1) * LANES] = jnp.concatenate([o[:, 0:BLOCK], o[:, BLOCK:2 * BLOCK]],
                                                                        axis=0).T
            for c in range(nc):
                scales[c][pair] = ws[c] * inv_l
        impsel_s[...] = jnp.zeros(impsel_s.shape, F32)
        for c in range(nc):
            r0 = c * BLOCK
            pn = e_s[r0:r0 + BLOCK, :] * jnp.concatenate(scales[c], axis=1)
            imp = pn[:, 0:BLOCK] + pn[:, BLOCK:2 * BLOCK] + pn[:, 2 * BLOCK:3 * BLOCK] + pn[:, 3 * BLOCK:4 * BLOCK]
            hi = imp.astype(BF16)
            r1 = imp - hi.astype(F32)
            mid = r1.astype(BF16)
            lo = (r1 - mid.astype(F32)).astype(BF16)
            b0 = c * (BLOCK // 4)
            impsel_s[b0:b0 + 64, :] += _dot(mtl, hi) + _dot(mtl, mid) + _dot(mtl, lo)

    n_ch = n_cmp // BLOCK
    if n_ch >= 4:
        need = qi // CMP_STRIDE + 1
        levels = sorted({n_ch // 4, n_ch // 2, 3 * n_ch // 4, n_ch})
        for li, level in enumerate(levels):
            lo = levels[li - 1] if li else 0

            @pl.when((need > lo) & (need <= level))
            def _(level=level):
                attend(level)
    else:
        attend(n_ch)

    imp_sel = impsel_s[0:n_blk, :]
    blk = lax.broadcasted_iota(jnp.int32, (n_blk, BLOCK), 0)
    cur = (qi * BLOCK + lax.broadcasted_iota(jnp.int32, (n_blk, BLOCK), 1)) >> 6
    forced = (blk == 0) | (blk == cur) | (blk == cur - 1)
    causal = blk <= cur
    score = jnp.where(forced, -jnp.inf, jnp.where(causal, imp_sel, NEG))
    blk_f = blk.astype(F32)
    chosen = jnp.where(forced | (cur < SEL_TOPK), 1.0, 0.0)
    for _ in range(SEL_TOPK - 3):
        mx = jnp.max(score, axis=0, keepdims=True)
        first = jnp.min(jnp.where(score == mx, blk_f, 1e9), axis=0, keepdims=True)
        pick = blk_f == first
        chosen = jnp.where(pick, 1.0, chosen)
        score = jnp.where(pick, -jnp.inf, score)
    mq_ref[...] = jnp.where(causal & (chosen > 0.5), 0.0, NEG).astype(BF16)


def _importance_matrix():
    mt = np.zeros((64, BLOCK), np.float32)
    ratio = SEL_BLOCK // CMP_STRIDE
    for o in range(-(CMP_BLOCK // CMP_STRIDE - 1), ratio):
        w = max(0, min(SEL_BLOCK, o * CMP_STRIDE + CMP_BLOCK) - max(0, o * CMP_STRIDE)) / CMP_BLOCK
        for r in range(BLOCK // ratio + 1):
            k = r * ratio + o
            if 0 <= k < BLOCK:
                mt[r, k] = w
    return jnp.asarray(mt, dtype=BF16)


def _cmp_attn(qt, kc, vct, seq):
    n_cmp = seq // CMP_STRIDE
    n_blk = max(LANES, seq // SEL_BLOCK)
    n_ch = n_cmp // BLOCK
    mt = _importance_matrix()
    kern = functools.partial(_cmp_attn_kernel, n_cmp=n_cmp, n_blk=n_blk)
    return pl.pallas_call(
        kern,
        out_shape=(jax.ShapeDtypeStruct((seq, 512), F32),
                   jax.ShapeDtypeStruct((2, seq // BLOCK, n_blk, BLOCK), BF16)),
        grid=(2, seq // BLOCK),
        in_specs=[
            pl.BlockSpec((4 * HEAD_DIM, BLOCK), lambda h, i: (h, i)),
            pl.BlockSpec((n_cmp, LANES), lambda h, i: (0, 0)),
            pl.BlockSpec((None, n_ch, HEAD_DIM, BLOCK), lambda h, i: (h, 0, 0, 0)),
            pl.BlockSpec((64, BLOCK), lambda h, i: (0, 0)),
        ],
        out_specs=(pl.BlockSpec((BLOCK, 256), lambda h, i: (i, h)),
                   pl.BlockSpec((None, None, n_blk, BLOCK), lambda h, i: (h, i, 0, 0))),
        scratch_shapes=[pltpu.VMEM((n_cmp, 4 * BLOCK), F32),
                        pltpu.VMEM((n_blk + 64, BLOCK), F32)],
        compiler_params=_cparams(("arbitrary", "arbitrary")),
        name="cmp_attn_topk",
    )(qt, kc, vct, mt)


def _sel_kernel(qi_ref, kt_ref, qt_ref, mq_ref, k_ref, vt_ref, e_ref, tab_ref, b31_ref, o_ref,
                qaug, m_s, acc_s, s_s, *, n_e):
    kvh = pl.program_id(0)
    step = pl.program_id(1)
    qi = qi_ref[step]
    kt = kt_ref[step]
    q_per_sub = SEL_SUB // SEL_TQ
    win_steps = LANES * SEL_BLOCK // SEL_TK
    ncol = 4 * SEL_TQ

    @pl.when(kt == 0)
    def _():
        m_s[...] = jnp.full(m_s.shape, -3e38, F32)
        acc_s[...] = jnp.zeros(acc_s.shape, F32)

    @pl.when(kt % win_steps == 0)
    def _():
        row_half = lax.broadcasted_iota(jnp.int32, (LANES, SEL_TQ), 0) >> 6
        mq = mq_ref[...]
        for h in range(4):
            qh = qt_ref[h * HEAD_DIM:(h + 1) * HEAD_DIM, :].astype(F32)
            both = jnp.concatenate([qh, qh], axis=0)
            qaug[0:LANES, h * SEL_TQ:(h + 1) * SEL_TQ] = jnp.where(row_half == kvh, both, 0.0).astype(BF16)
            qaug[LANES:2 * LANES, h * SEL_TQ:(h + 1) * SEL_TQ] = mq

    sub_near0 = jnp.maximum(0, (qi - SEL_SLAB_HI + q_per_sub - 1) // q_per_sub)

    def run(with_bias, n_sub):
        pcols = [slice(pair * 2 * SEL_TQ, (pair + 1) * 2 * SEL_TQ) for pair in range(2)]
        qas = [qaug[:, cols] for cols in pcols]
        m_runs = [m_s[:, cols] for cols in pcols]
        if with_bias:
            b31s = [jnp.concatenate([jnp.full((1, SEL_TQ), b31_ref[kvh * 4 + pair * 2 + hh], F32)
                                     for hh in range(2)], axis=1) for pair in range(2)]
            m_runs = [m + jnp.where(sub_near0 >= kt * SEL_NSUB, b, 0.0) for m, b in zip(m_runs, b31s)]
        colmax = [jnp.full((8, 2 * SEL_TQ), -3e38, F32) for _ in range(2)]
        n_chunk = n_sub * SEL_SUB // SEL_CHUNK

        def score_chunk(pair, idx):
            r0 = idx * SEL_CHUNK
            j, c = divmod(idx, SEL_SUB // SEL_CHUNK)
            sub = kt * SEL_NSUB + j
            is_far = sub < sub_near0
            kaug = jnp.concatenate([k_ref[r0:r0 + SEL_CHUNK, :],
                                    e_ref[sub % n_e, c * SEL_CHUNK:(c + 1) * SEL_CHUNK, :]], axis=1)
            s = _dot(kaug, qas[pair])
            if with_bias:
                far_shift = jnp.where(is_far, b31s[pair], 0.0)
                blocks = []
                for hh in range(2):
                    col = []
                    for tt in range(SEL_CHUNK // BLOCK):
                        t = c * (SEL_CHUNK // BLOCK) + tt
                        u = jnp.maximum(qi - q_per_sub * sub - t, SEL_SLAB_LO)
                        uidx = jnp.where(is_far, SEL_NSLAB - 1, u - SEL_SLAB_LO)
                        col.append(s[tt * BLOCK:(tt + 1) * BLOCK, hh * SEL_TQ:(hh + 1) * SEL_TQ]
                                   + tab_ref[pair * 2 + hh, uidx])
                    blocks.append(jnp.concatenate(col, axis=0))
                s = jnp.concatenate(blocks, axis=1) + far_shift
            s_s[r0:r0 + SEL_CHUNK, pcols[pair]] = s
            for g in range(SEL_CHUNK // 8):
                colmax[pair] = jnp.maximum(colmax[pair], s[g * 8:(g + 1) * 8, :])

        for idx in range(n_chunk):
            score_chunk(0, idx)
            score_chunk(1, idx)
        m_news = [jnp.maximum(m, jnp.max(cm, axis=0, keepdims=True)) for m, cm in zip(m_runs, colmax)]
        accs = [jnp.exp(m_runs[pair] - m_news[pair]) * acc_s[:, pcols[pair]] for pair in range(2)]
        for idx in range(n_chunk):
            r0 = idx * SEL_CHUNK
            vt = vt_ref[0:SEL_VROWS, r0:r0 + SEL_CHUNK]
            for pair in range(2):
                pr = jnp.exp(s_s[r0:r0 + SEL_CHUNK, pcols[pair]] - m_news[pair]).astype(BF16)
                accs[pair] = accs[pair] + _dot(vt, pr)
        for pair in range(2):
            acc_s[:, pcols[pair]] = accs[pair]
            m_s[:, pcols[pair]] = m_news[pair]

    all_far = kt * SEL_NSUB + SEL_NSUB - 1 < sub_near0
    n_live = jnp.minimum(SEL_NSUB, qi // q_per_sub - kt * SEL_NSUB + 1)

    @pl.when(all_far)
    def _():
        run(False, SEL_NSUB)

    for live in range(1, SEL_NSUB + 1):
        @pl.when(jnp.logical_not(all_far) & (n_live == live))
        def _(live=live):
            run(True, live)

    @pl.when(kt == qi // (SEL_TK // SEL_TQ))
    def _():
        acc = acc_s[...]
        o = acc[0:HEAD_DIM, :] / acc[HEAD_DIM:HEAD_DIM + 1, :]
        for jp in range(2):
            blk = jnp.concatenate([o[:, (2 * jp) * SEL_TQ:(2 * jp + 1) * SEL_TQ],
                                   o[:, (2 * jp + 1) * SEL_TQ:(2 * jp + 2) * SEL_TQ]], axis=0)
            o_ref[:, jp * LANES:(jp + 1) * LANES] = blk.T


def _sel_attn(qkv, qt, vt, mq, bias_s, rel_bias, seq):
    n_q = seq // SEL_TQ
    per_step = SEL_TK // SEL_TQ
    qi_l, kt_l = [], []
    for qi in range(n_q):
        for kt in range(qi // per_step + 1):
            qi_l.append(qi)
            kt_l.append(kt)
    qi_arr = jnp.asarray(np.array(qi_l, np.int32))
    kt_arr = jnp.asarray(np.array(kt_l, np.int32))
    win_steps = LANES * SEL_BLOCK // SEL_TK
    n_e = min(LANES * SEL_BLOCK // SEL_SUB, seq // SEL_SUB)
    e_np = np.zeros((n_e, SEL_SUB, LANES), np.float32)
    for t in range(n_e):
        for k in range(SEL_SUB):
            e_np[t, k, (t * (SEL_SUB // SEL_BLOCK) + k // SEL_BLOCK) % LANES] = 1.0
    e_all = jnp.asarray(e_np, dtype=BF16)
    b31 = rel_bias[REL_BUCKETS - 1, 4:12]
    grid_spec = pltpu.PrefetchScalarGridSpec(
        num_scalar_prefetch=2,
        grid=(2, len(qi_l)),
        in_specs=[
            pl.BlockSpec((4 * HEAD_DIM, SEL_TQ), lambda h, s, qi, kt: (h, qi[s])),
            pl.BlockSpec((None, None, LANES, SEL_TQ), lambda h, s, qi, kt: (h, qi[s], kt[s] // win_steps, 0)),
            pl.BlockSpec((SEL_TK, LANES), lambda h, s, qi, kt: (kt[s], _qkv_col(COL_CKS) // LANES)),
            pl.BlockSpec((None, LANES, SEL_TK), lambda h, s, qi, kt: (h, 0, kt[s])),
            pl.BlockSpec((n_e, SEL_SUB, LANES), lambda h, s, qi, kt: (0, 0, 0)),
            pl.BlockSpec((4, SEL_NSLAB, BLOCK, BLOCK), lambda h, s, qi, kt: (h, 0, 0, 0)),
            pl.BlockSpec(memory_space=pltpu.SMEM),
        ],
        out_specs=pl.BlockSpec((SEL_TQ, 256), lambda h, s, qi, kt: (qi[s], h)),
        scratch_shapes=[pltpu.VMEM((2 * LANES, 4 * SEL_TQ), BF16),
                        pltpu.VMEM((1, 4 * SEL_TQ), F32),
                        pltpu.VMEM((SEL_VROWS, 4 * SEL_TQ), F32),
                        pltpu.VMEM((SEL_TK, 4 * SEL_TQ), F32)],
    )
    return pl.pallas_call(
        functools.partial(_sel_kernel, n_e=n_e),
        out_shape=jax.ShapeDtypeStruct((seq, 512), F32),
        grid_spec=grid_spec,
        compiler_params=_cparams(("arbitrary", "arbitrary")),
        name="selected_attn",
    )(qi_arr, kt_arr, qt, mq, qkv, vt, e_all, bias_s, b31)


def _layer_norm(z, g, b):
    mu = jnp.mean(z, axis=-1, keepdims=True)
    zc = z - mu
    var = jnp.mean(jnp.square(zc), axis=-1, keepdims=True)
    return zc * lax.rsqrt(var + LN_EPS) * g + b


def _outproj_kernel(x_ref, oa0, oa1, oa2, la0, la1, la2, ob_ref, ocmp_ref, oslc_ref, owin_ref, g_ref,
                    gx_ref, w_ref, ga_ref, lng_ref, lnb_ref, o_ref):
    l0, l1, l2 = la0[...], la1[...], la2[...]
    mx = jnp.maximum(jnp.maximum(l0, l1), l2)
    w0, w1, w2 = jnp.exp(l0 - mx), jnp.exp(l1 - mx), jnp.exp(l2 - mx)
    oa = (w0 * oa0[...] + w1 * oa1[...] + w2 * oa2[...]) / (w0 + w1 + w2)
    gt = jax.nn.sigmoid(g_ref[...])
    hi = gt.astype(BF16)
    lo = (gt - hi.astype(F32)).astype(BF16)
    gx = gx_ref[...]
    gates = _dot(hi, gx) + _dot(lo, gx)
    oc = gates[:, 0:512] * ocmp_ref[...] + gates[:, 512:1024] * oslc_ref[...] + gates[:, 1024:1536] * owin_ref[...]
    mixed = jnp.concatenate([oa, ob_ref[...], oc], axis=1).astype(BF16)
    y = _dot(mixed, w_ref[...])
    z = DN_ALPHA * x_ref[...] + (1 + ga_ref[...]) * y
    o_ref[...] = _layer_norm(z, lng_ref[...], lnb_ref[...])


def _gate_expand():
    gx = np.zeros((LANES, 3 * 512), np.float32)
    for h in range(8):
        for b in range(3):
            gx[h * 3 + b, b * 512 + h * HEAD_DIM:b * 512 + (h + 1) * HEAD_DIM] = 1.0
    return jnp.asarray(gx, dtype=BF16)


def _outproj(x, oa, la, ob, ocmp, oslc, owin, gates, w_out_l, mod_l, lng, lnb, seq):
    tm = 256
    row = lambda w: pl.BlockSpec((tm, w), lambda i: (i, 0))
    vec = lambda k: pl.BlockSpec((1, D_MODEL), lambda i, k=k: (0, k))
    return pl.pallas_call(
        _outproj_kernel,
        out_shape=jax.ShapeDtypeStruct((seq, D_MODEL), F32),
        grid=(seq // tm,),
        in_specs=[row(D_MODEL)] + [row(256)] * 7 + [row(512)] * 3 + [row(LANES)] + [
            pl.BlockSpec((LANES, 3 * 512), lambda i: (0, 0)),
            pl.BlockSpec((D_MODEL, D_MODEL), lambda i: (0, 0)),
            vec(2), vec(0), vec(0)],
        out_specs=row(D_MODEL),
        compiler_params=_cparams(("arbitrary",)),
        name="out_proj_ln",
    )(x, oa[0], oa[1], oa[2], la[0], la[1], la[2], ob, ocmp, oslc, owin, gates, _gate_expand(), w_out_l,
      mod_l, lng, lnb)


def _mlp_kernel(x_ref, sc_ref, sh_ref, g_ref, w1_ref, w2_ref, lng_ref, lnb_ref, o_ref, h_s, acc_s):
    j = pl.program_id(1)

    @pl.when(j == 0)
    def _():
        h_s[...] = (x_ref[...] * (1 + sc_ref[...]) + sh_ref[...]).astype(BF16)
        acc_s[...] = jnp.zeros(acc_s.shape, F32)

    f = jnp.maximum(_dot(h_s[...], w1_ref[...]), 0.0)
    acc_s[...] += _dot(jnp.square(f).astype(BF16), w2_ref[...])

    @pl.when(j == pl.num_programs(1) - 1)
    def _():
        z = DN_ALPHA * x_ref[...] + (1 + g_ref[...]) * acc_s[...]
        o_ref[...] = _layer_norm(z, lng_ref[...], lnb_ref[...])


def _mlp(x, mod_l, w1_l, w2_l, lng, lnb, seq):
    tm = 1024
    tf = 1024
    vec = lambda k: pl.BlockSpec((1, D_MODEL), lambda i, j, k=k: (0, k))
    return pl.pallas_call(
        _mlp_kernel,
        out_shape=jax.ShapeDtypeStruct((seq, D_MODEL), F32),
        grid=(seq // tm, D_FF // tf),
        in_specs=[
            pl.BlockSpec((tm, D_MODEL), lambda i, j: (i, 0)),
            vec(4), vec(3), vec(5),
            pl.BlockSpec((D_MODEL, tf), lambda i, j: (0, j)),
            pl.BlockSpec((tf, D_MODEL), lambda i, j: (j, 0)),
            vec(0), vec(0),
        ],
        out_specs=pl.BlockSpec((tm, D_MODEL), lambda i, j: (i, 0)),
        scratch_shapes=[pltpu.VMEM((tm, D_MODEL), BF16), pltpu.VMEM((tm, D_MODEL), F32)],
        compiler_params=_cparams(("arbitrary", "arbitrary")),
        name="mlp_ln",
    )(x, mod_l, mod_l, mod_l, w1_l, w2_l, lng, lnb)


def kernel(x, c, positions, w_in, w_out, rel_bias, sinks, cmp_pos, cmp_w1, cmp_w2, mlp_w1, mlp_w2,
           ada_w, ada_b, ln_g, ln_b):
    bsz, seq, d = x.shape
    assert bsz == 1 and d == D_MODEL and seq % (2 * A_CONFIGS[-1][0]) == 0
    x = x.reshape(seq, D_MODEL)

    mod = _modulation(c, ada_w, ada_b)
    cos_t, sin_t = _rope_tables(positions, seq)
    bias_a, bias_w, bias_s = _bias_tables(rel_bias)

    w_in_b = jnp.pad(w_in, ((0, 0), (0, 0), (0, N_IN_PAD - N_IN))).astype(BF16)
    w_out_b = w_out.astype(BF16)
    w1_b = mlp_w1.astype(BF16)
    w2_b = mlp_w2.astype(BF16)
    cw1_b = cmp_w1.astype(BF16)
    cw2_b = cmp_w2.astype(BF16)

    for l in range(DEPTH):
        mod_l = mod[l]
        qkv, ck, gates, qt, vt, a_qkv = _inproj(x, mod_l, w_in_b[l], cos_t, sin_t, seq)

        oa, la = [], []
        for ci, (window, dil) in enumerate(A_CONFIGS):
            nrows = seq // dil
            view = a_qkv.reshape(nrows, dil * N_A)
            o_c, l_c = _banded_call(
                view, nrows, dil, N_A // LANES, 0, 2, 4, 2,
                nprev=1, tq=min(BAND_TQ, nrows), max_dist=window // dil, bias=bias_a[ci], want_lse=True,
                group_div=None, kv_per_qblock=True, name=f"dilated_attn_{dil}")
            oa.append(o_c.reshape(seq, 256))
            la.append(l_c.reshape(seq, 256))

        (ob,) = _banded_call(
            qkv, seq, 1, N_QKV // LANES, _qkv_col(COL_BQ) // LANES, _qkv_col(COL_BK) // LANES,
            _qkv_col(COL_BV) // LANES, 2,
            nprev=1, tq=BAND_TQ, max_dist=B_WINDOW - 1, sinks=sinks[l], group_div=1,
            kv_per_qblock=False, name="swa_attn")

        kc, vct = _compress(ck, cmp_pos[l], cw1_b[l], cw2_b[l], seq)
        ocmp, mq = _cmp_attn(qt, kc, vct, seq)
        oslc = _sel_attn(qkv, qt, vt, mq, bias_s, rel_bias, seq)
        (owin,) = _banded_call(
            qkv, seq, 1, N_QKV // LANES, _qkv_col(COL_CQ) // LANES, _qkv_col(COL_CKW) // LANES,
            _qkv_col(COL_CVW) // LANES, 4,
            nprev=C_WINDOW // BLOCK, tq=BAND_TQ, max_dist=C_WINDOW - 1, bias=bias_w, group_div=2,
            kv_per_qblock=False, name="window_attn")

        x = _outproj(x, oa, la, ob, ocmp, oslc, owin, gates, w_out_b[l], mod_l,
                     ln_g[l, 0:1], ln_b[l, 0:1], seq)
        x = _mlp(x, mod_l, w1_b[l], w2_b[l], ln_g[l, 1:2], ln_b[l, 1:2], seq)

    return x.reshape(bsz, seq, D_MODEL)
```

```python
import functools
import math

import numpy as np
import jax
import jax.numpy as jnp
from jax import lax
from jax.experimental import pallas as pl
from jax.experimental.pallas import tpu as pltpu

F32 = jnp.float32
BF16 = jnp.bfloat16

D_MODEL = 1024
DEPTH = 4
HEAD_DIM = 64
LANES = 128
BLOCK = 128
A_CONFIGS = ((128, 1), (512, 4), (2048, 16))
B_WINDOW = 128
ROPE_THETA = 150000.0
CMP_BLOCK = 32
CMP_STRIDE = 16
CMP_HIDDEN = 256
SEL_BLOCK = 64
SEL_TOPK = 16
C_WINDOW = 512
REL_BUCKETS = 32
REL_MAX_DIST = 2048
D_FF = 4 * D_MODEL
DN_ALPHA = (2 * DEPTH) ** 0.25
LN_EPS = 1e-5
NEG = -1e30
FORCE = 1e4
SCALE = HEAD_DIM ** -0.5

COL_AQ, COL_AK, COL_AV = 0, 256, 512
COL_BQ, COL_BK, COL_BV = 768, 1024, 1152
COL_CQ = 1280
COL_CKC = 1792
COL_CKS, COL_CVS = 2048, 2176
COL_CKW, COL_CVW = 2304, 2432
COL_CG = 2560
N_A = 768
N_QKV = 1536


def _qkv_col(col):
    assert COL_BQ <= col < COL_CKC or COL_CKS <= col < COL_CG
    return col - N_A if col < COL_CKC else col - N_A - 256
N_IN = 2584
N_IN_PAD = 2688

BAND_TQ = 1024
SEL_TQ = 128
SEL_SUB = 512
SEL_CHUNK = 128
SEL_VROWS = 80
SEL_NSUB = 4
SEL_TK = SEL_SUB * SEL_NSUB
SEL_SLAB_LO = -1
SEL_SLAB_HI = 15
SEL_NSLAB = SEL_SLAB_HI - SEL_SLAB_LO + 2
VMEM_LIMIT = 56 * 1024 * 1024


def _cparams(sem, flags=None):
    return pltpu.CompilerParams(dimension_semantics=sem, vmem_limit_bytes=VMEM_LIMIT, flags=flags)


def _dot_nt(a, b):
    return lax.dot_general(a, b, (((1,), (1,)), ((), ())), preferred_element_type=F32)


def _dot(a, b):
    return jnp.dot(a, b, preferred_element_type=F32)


def _mod_kernel(c_ref, w_ref, b_ref, o_ref):
    c = c_ref[...]
    act = c * jax.nn.sigmoid(c)
    lhs = jnp.broadcast_to(act, (8, D_MODEL)).astype(BF16)
    y = _dot(lhs, w_ref[...].astype(BF16))
    o_ref[...] = y[0:1, :] + b_ref[...]


def _modulation(c, ada_w, ada_b):
    tn = 1536
    return pl.pallas_call(
        _mod_kernel,
        out_shape=jax.ShapeDtypeStruct((DEPTH, 1, 6 * D_MODEL), F32),
        grid=(DEPTH, 6 * D_MODEL // tn),
        in_specs=[
            pl.BlockSpec((1, D_MODEL), lambda l, j: (0, 0)),
            pl.BlockSpec((None, D_MODEL, tn), lambda l, j: (l, 0, j)),
            pl.BlockSpec((None, 1, tn), lambda l, j: (l, 0, j)),
        ],
        out_specs=pl.BlockSpec((None, 1, tn), lambda l, j: (l, 0, j)),
        compiler_params=_cparams(("arbitrary", "arbitrary")),
        name="adaln_mod",
    )(c, ada_w, ada_b.reshape(DEPTH, 1, 6 * D_MODEL))


def _rope_table_kernel(pos_ref, freq_ref, cos_ref, sin_ref):
    ang = pos_ref[...].astype(F32) * freq_ref[...]
    cos_ref[...] = jnp.cos(ang)
    sin_ref[...] = jnp.sin(ang)


def _rope_tables(positions, seq):
    half = HEAD_DIM // 2
    freq = ROPE_THETA ** (-jnp.arange(half, dtype=F32) / half)
    freq = jnp.tile(freq, LANES // half).reshape(1, LANES)
    tm = 1024
    return pl.pallas_call(
        _rope_table_kernel,
        out_shape=(jax.ShapeDtypeStruct((seq, LANES), F32),) * 2,
        grid=(seq // tm,),
        in_specs=[pl.BlockSpec((tm, 1), lambda i: (i, 0)), pl.BlockSpec((1, LANES), lambda i: (0, 0))],
        out_specs=(pl.BlockSpec((tm, LANES), lambda i: (i, 0)),) * 2,
        compiler_params=_cparams(("arbitrary",)),
        name="rope_tables",
    )(positions.reshape(seq, 1), freq)


def _t5_bias(rb_ref, dist, col, max_dist):
    exact = REL_BUCKETS // 2
    d = jnp.maximum(dist, 0)
    df = jnp.maximum(d, 1).astype(F32)
    large = exact + (jnp.log(df / exact) / math.log(REL_MAX_DIST / exact)
                     * (REL_BUCKETS - exact)).astype(jnp.int32)
    large = jnp.minimum(large, REL_BUCKETS - 1)
    bucket = jnp.where(d < exact, d, large)
    out = jnp.zeros(dist.shape, F32)
    for b in range(REL_BUCKETS):
        out = jnp.where(bucket == b, rb_ref[b * 12 + col], out)
    return out, bucket


def _bias_a_kernel(rb_ref, o_ref):
    cfg = pl.program_id(0)
    head = pl.program_id(1)
    dil = jnp.where(cfg == 0, 1, jnp.where(cfg == 1, 4, 16))
    r = lax.broadcasted_iota(jnp.int32, (BLOCK, 2 * BLOCK), 0)
    c = lax.broadcasted_iota(jnp.int32, (BLOCK, 2 * BLOCK), 1)
    dist = r + BLOCK - c
    bias, _ = _t5_bias(rb_ref, dist * dil, head, None)
    o_ref[...] = jnp.where((dist >= 0) & (dist <= BLOCK), bias, NEG)


def _bias_w_kernel(rb_ref, o_ref):
    head = pl.program_id(0)
    nprev = C_WINDOW // BLOCK
    r = lax.broadcasted_iota(jnp.int32, (BLOCK, (nprev + 1) * BLOCK), 0)
    c = lax.broadcasted_iota(jnp.int32, (BLOCK, (nprev + 1) * BLOCK), 1)
    dist = r + nprev * BLOCK - c
    bias, _ = _t5_bias(rb_ref, dist, 4 + head, None)
    o_ref[...] = jnp.where((dist >= 0) & (dist <= C_WINDOW - 1), bias, NEG)


def _bias_s_kernel(rb_ref, o_ref):
    head = pl.program_id(0)
    slab = pl.program_id(1)
    r = lax.broadcasted_iota(jnp.int32, (BLOCK, BLOCK), 1)
    c = lax.broadcasted_iota(jnp.int32, (BLOCK, BLOCK), 0)
    dist = (slab + SEL_SLAB_LO) * BLOCK + r - c
    bias, _ = _t5_bias(rb_ref, dist, 4 + head, None)
    val = jnp.where(dist >= 0, bias, NEG)
    o_ref[...] = jnp.where(slab == SEL_NSLAB - 1, 0.0, val)


def _bias_tables(rel_bias):
    rb = rel_bias.reshape(-1)
    smem = pl.BlockSpec(memory_space=pltpu.SMEM)
    bias_a = pl.pallas_call(
        _bias_a_kernel,
        out_shape=jax.ShapeDtypeStruct((3, 4, BLOCK, 2 * BLOCK), F32),
        grid=(3, 4),
        in_specs=[smem],
        out_specs=pl.BlockSpec((None, None, BLOCK, 2 * BLOCK), lambda a, h: (a, h, 0, 0)),
        compiler_params=_cparams(("arbitrary", "arbitrary")),
        name="bias_table_dilated",
    )(rb)
    wctx = C_WINDOW + BLOCK
    bias_w = pl.pallas_call(
        _bias_w_kernel,
        out_shape=jax.ShapeDtypeStruct((8, BLOCK, wctx), F32),
        grid=(8,),
        in_specs=[smem],
        out_specs=pl.BlockSpec((None, BLOCK, wctx), lambda h: (h, 0, 0)),
        compiler_params=_cparams(("arbitrary",)),
        name="bias_table_window",
    )(rb)
    bias_s = pl.pallas_call(
        _bias_s_kernel,
        out_shape=jax.ShapeDtypeStruct((8, SEL_NSLAB, BLOCK, BLOCK), F32),
        grid=(8, SEL_NSLAB),
        in_specs=[smem],
        out_specs=pl.BlockSpec((None, None, BLOCK, BLOCK), lambda h, u: (h, u, 0, 0)),
        compiler_params=_cparams(("arbitrary", "arbitrary")),
        name="bias_table_selected",
    )(rb)
    return bias_a, bias_w, bias_s


def _rope_apply(t, cos, sin, first):
    up = pltpu.roll(t, LANES - HEAD_DIM // 2, axis=1)
    dn = pltpu.roll(t, HEAD_DIM // 2, axis=1)
    return t * cos + jnp.where(first, -up, dn) * sin


def _inproj_kernel(x_ref, sc_ref, sh_ref, w_ref, cos_ref, sin_ref, qkv_ref, ck_ref, g_ref, qt_ref, vt_ref,
                   a_ref):
    h = (x_ref[...] * (1 + sc_ref[...]) + sh_ref[...]).astype(BF16)
    y = _dot(h, w_ref[...])
    cos = cos_ref[...]
    sin = sin_ref[...]
    lane = lax.broadcasted_iota(jnp.int32, cos.shape, 1)
    first = (lane & (HEAD_DIM - 1)) < HEAD_DIM // 2

    def put(c0, width, val):
        q0 = _qkv_col(c0)
        qkv_ref[:, q0:q0 + width] = val.astype(BF16)

    a_ref[:, 0:256] = (y[:, COL_AQ:COL_AQ + 256] * SCALE).astype(BF16)
    a_ref[:, 256:768] = y[:, COL_AK:COL_AK + 512].astype(BF16)
    for j in range(2):
        c0 = COL_BQ + j * LANES
        put(c0, LANES, _rope_apply(y[:, c0:c0 + LANES], cos, sin, first) * SCALE)
    put(COL_BK, LANES, _rope_apply(y[:, COL_BK:COL_BK + LANES], cos, sin, first))
    put(COL_BV, LANES, y[:, COL_BV:COL_BV + LANES])
    put(COL_CQ, 512, y[:, COL_CQ:COL_CQ + 512] * SCALE)
    put(COL_CKS, 512, y[:, COL_CKS:COL_CKS + 512])
    for g in range(4):
        c0 = COL_CKC + g * HEAD_DIM
        ck_ref[g] = y[:, c0:c0 + HEAD_DIM]
    g_ref[...] = y[:, COL_CG:COL_CG + LANES]
    qt_ref[...] = (y[:, COL_CQ:COL_CQ + 512] * SCALE).T.astype(BF16)
    vt = y[:, COL_CVS:COL_CVS + LANES].T
    ones_row = (lax.broadcasted_iota(jnp.int32, (HEAD_DIM, vt.shape[1]), 0) == 0).astype(F32)
    for kvh in range(2):
        vt_ref[kvh] = jnp.concatenate([vt[kvh * HEAD_DIM:(kvh + 1) * HEAD_DIM, :], ones_row], axis=0).astype(BF16)


def _inproj(x, mod_l, w_in_l, cos_t, sin_t, seq):
    tm = 512
    vec = lambda k: pl.BlockSpec((1, D_MODEL), lambda i, k=k: (0, k))
    return pl.pallas_call(
        _inproj_kernel,
        out_shape=(jax.ShapeDtypeStruct((seq, N_QKV), BF16),
                   jax.ShapeDtypeStruct((4, seq, HEAD_DIM), F32),
                   jax.ShapeDtypeStruct((seq, LANES), F32),
                   jax.ShapeDtypeStruct((512, seq), BF16),
                   jax.ShapeDtypeStruct((2, LANES, seq), BF16),
                   jax.ShapeDtypeStruct((seq, N_A), BF16)),
        grid=(seq // tm,),
        in_specs=[
            pl.BlockSpec((tm, D_MODEL), lambda i: (i, 0)),
            vec(1), vec(0),
            pl.BlockSpec((D_MODEL, N_IN_PAD), lambda i: (0, 0)),
            pl.BlockSpec((tm, LANES), lambda i: (i, 0)),
            pl.BlockSpec((tm, LANES), lambda i: (i, 0)),
        ],
        out_specs=(pl.BlockSpec((tm, N_QKV), lambda i: (i, 0)),
                   pl.BlockSpec((4, tm, HEAD_DIM), lambda i: (0, i, 0)),
                   pl.BlockSpec((tm, LANES), lambda i: (i, 0)),
                   pl.BlockSpec((512, tm), lambda i: (0, i)),
                   pl.BlockSpec((2, LANES, tm), lambda i: (0, 0, i)),
                   pl.BlockSpec((tm, N_A), lambda i: (i, 0))),
        compiler_params=_cparams(("arbitrary",)),
        name="in_proj",
    )(x, mod_l, mod_l, w_in_l, cos_t, sin_t)


def _lane_half():
    return lax.broadcasted_iota(jnp.int32, (BLOCK, LANES), 1) >> 6


def _align_q(q, half, x, ysel):
    qx = jnp.where(half == x, q, 0.0)
    if ysel is None:
        return qx.astype(BF16)
    qd = qx + pltpu.roll(qx, HEAD_DIM, axis=1)
    return jnp.where(ysel, qd, 0.0).astype(BF16)


def _spread_o(o, ysel):
    if ysel is None:
        return o
    ob = jnp.where(ysel, o, 0.0)
    return ob + pltpu.roll(ob, HEAD_DIM, axis=1)


def _banded_kernel(*refs, nprev, tq, max_dist, use_bias, use_sinks, want_lse, group_div, row_axis):
    refs = list(refs)
    sink_ref = refs.pop(0) if use_sinks else None
    q_ref, kp_ref, kc_ref, vp_ref, vc_ref = refs[:5]
    refs = refs[5:]
    bias_ref = refs.pop(0) if use_bias else None
    o_ref = refs.pop(0)
    lse_ref = refs.pop(0) if want_lse else None
    kctx, vctx = refs

    tp = nprev * BLOCK
    ctx = (nprev + 1) * BLOCK
    i = pl.program_id(row_axis)
    p = pl.program_id(row_axis - 1)
    kctx[0:tp, :] = kp_ref[...]
    kctx[tp:tp + tq, :] = kc_ref[...]
    vctx[0:tp, :] = vp_ref[...]
    vctx[tp:tp + tq, :] = vc_ref[...]

    half = _lane_half()
    upper = half == 1
    ysel = None if group_div is None else half == p // group_div
    chunks = [(c0, min(c0 + 2 * BLOCK, ctx)) for c0 in range(0, ctx, 2 * BLOCK)]
    rrs, ccs, bands = [], [], []
    for c0, c1 in chunks:
        rr = lax.broadcasted_iota(jnp.int32, (BLOCK, c1 - c0), 0)
        cc = lax.broadcasted_iota(jnp.int32, (BLOCK, c1 - c0), 1) + c0
        ccs.append(cc)
        if not use_bias:
            dist = rr + tp - cc
            bands.append((dist >= 0) & (dist <= max_dist))

    for sub in range(tq // BLOCK):
        rows = slice(sub * BLOCK, (sub + 1) * BLOCK)
        q = q_ref[rows, :].astype(F32)
        low = tp - (i * tq + sub * BLOCK)
        outs, lses = [], []
        q2 = jnp.concatenate([_align_q(q, half, x, ysel) for x in range(2)], axis=0)
        stats = [([], [], []), ([], [], [])]
        for ci, (c0, c1) in enumerate(chunks):
            k = kctx[sub * BLOCK + c0:sub * BLOCK + c1, :]
            v = vctx[sub * BLOCK + c0:sub * BLOCK + c1, :]
            s2 = _dot_nt(q2, k)
            es = []
            for x in range(2):
                s = s2[x * BLOCK:(x + 1) * BLOCK, :]
                if use_bias:
                    s = s + bias_ref[x, :, c0:c1]
                else:
                    s = jnp.where(bands[ci], s, NEG)
                s = jnp.where(ccs[ci] >= low, s, NEG)
                m_loc = jnp.max(s, axis=1, keepdims=True)
                e = jnp.exp(s - m_loc)
                stats[x][0].append(m_loc)
                stats[x][1].append(jnp.sum(e, axis=1, keepdims=True))
                es.append(e.astype(BF16))
            pv2 = _dot(jnp.concatenate(es, axis=0), v)
            for x in range(2):
                stats[x][2].append(pv2[x * BLOCK:(x + 1) * BLOCK, :])
        for x in range(2):
            ms, dens, pvs = stats[x]
            m = ms[0]
            for m_loc in ms[1:]:
                m = jnp.maximum(m, m_loc)
            if use_sinks:
                sk = sink_ref[2 * p + x]
                m = jnp.maximum(m, sk)
            ws = [jnp.exp(m_loc - m) for m_loc in ms]
            den = ws[0] * dens[0]
            acc = ws[0] * pvs[0]
            for w, d, pv in zip(ws[1:], dens[1:], pvs[1:]):
                den = den + w * d
                acc = acc + w * pv
            if use_sinks:
                den = den + jnp.exp(sk - m)
            o = acc / den
            outs.append(_spread_o(o, ysel))
            lses.append(m + jnp.log(den))
        o_ref[rows, :] = jnp.where(upper, outs[1], outs[0])
        if want_lse:
            lse_ref[rows, :] = jnp.where(upper, jnp.broadcast_to(lses[1], (BLOCK, LANES)),
                                         jnp.broadcast_to(lses[0], (BLOCK, LANES)))


def _banded_call(qkv_view, nrows, nsub, row_width_blocks, q_cb, k_cb, v_cb, n_qblocks, *, nprev, tq,
                 max_dist, bias=None, sinks=None, want_lse=False, group_div=None, kv_per_qblock, name):
    tp = nprev * BLOCK
    ratio = tq // tp
    ctx = (nprev + 1) * BLOCK

    def kvb(p):
        return p if kv_per_qblock else 0

    in_specs = []
    args = []
    if sinks is not None:
        in_specs.append(pl.BlockSpec(memory_space=pltpu.SMEM))
        args.append(sinks)
    in_specs += [
        pl.BlockSpec((tq, LANES), lambda r, p, i: (i, r * row_width_blocks + q_cb + p)),
        pl.BlockSpec((tp, LANES), lambda r, p, i: (jnp.maximum(i * ratio - 1, 0), r * row_width_blocks + k_cb + kvb(p))),
        pl.BlockSpec((tq, LANES), lambda r, p, i: (i, r * row_width_blocks + k_cb + kvb(p))),
        pl.BlockSpec((tp, LANES), lambda r, p, i: (jnp.maximum(i * ratio - 1, 0), r * row_width_blocks + v_cb + kvb(p))),
        pl.BlockSpec((tq, LANES), lambda r, p, i: (i, r * row_width_blocks + v_cb + kvb(p))),
    ]
    args += [qkv_view] * 5
    if bias is not None:
        in_specs.append(pl.BlockSpec((2, BLOCK, ctx), lambda r, p, i: (p, 0, 0)))
        args.append(bias)
    out_w = nsub * n_qblocks * LANES
    o_spec = pl.BlockSpec((tq, LANES), lambda r, p, i: (i, r * n_qblocks + p))
    out_shape = [jax.ShapeDtypeStruct((nrows, out_w), F32)]
    out_specs = [o_spec]
    if want_lse:
        out_shape.append(jax.ShapeDtypeStruct((nrows, out_w), F32))
        out_specs.append(o_spec)
    kern = functools.partial(_banded_kernel, nprev=nprev, tq=tq, max_dist=max_dist, use_bias=bias is not None,
                             use_sinks=sinks is not None, want_lse=want_lse, group_div=group_div, row_axis=2)
    return pl.pallas_call(
        kern,
        out_shape=tuple(out_shape),
        grid=(nsub, n_qblocks, nrows // tq),
        in_specs=in_specs,
        out_specs=tuple(out_specs),
        scratch_shapes=[pltpu.VMEM((tp + tq, LANES), BF16), pltpu.VMEM((tp + tq, LANES), BF16)],
        compiler_params=_cparams(("arbitrary", "arbitrary", "arbitrary")),
        name=name,
    )(*args)


def _compress_kernel(x_ref, pos_ref, w1_ref, w2_ref, o_ref):
    x = x_ref[...]
    half = CMP_STRIDE * HEAD_DIM
    xa = (x + pos_ref[0:1, :]).astype(BF16)
    xb = (x + pos_ref[1:2, :]).astype(BF16)
    first = _dot(xa, w1_ref[0:half, :])
    second = _dot(xb, w1_ref[half:2 * half, :])
    n = x.shape[0]
    hid = first + pltpu.roll(second, n - 1, axis=0)
    act = jax.nn.gelu(hid)
    o_ref[...] = _dot(act.astype(BF16), w2_ref[...]).astype(BF16)


def _compress(ck, cmp_pos_l, w1_l, w2_l, seq):
    n_chunk = seq // CMP_STRIDE
    feat = CMP_STRIDE * HEAD_DIM
    x = ck.reshape(4, n_chunk, feat)
    pos = cmp_pos_l.reshape(2, 2, feat)
    out = pl.pallas_call(
        _compress_kernel,
        out_shape=jax.ShapeDtypeStruct((4, n_chunk, HEAD_DIM), BF16),
        grid=(4,),
        in_specs=[
            pl.BlockSpec((None, n_chunk, feat), lambda g: (g, 0, 0)),
            pl.BlockSpec((None, 2, feat), lambda g: (g // 2, 0, 0)),
            pl.BlockSpec((None, 2 * feat, CMP_HIDDEN), lambda g: (g // 2, 0, 0)),
            pl.BlockSpec((None, CMP_HIDDEN, HEAD_DIM), lambda g: (g // 2, 0, 0)),
        ],
        out_specs=pl.BlockSpec((None, n_chunk, HEAD_DIM), lambda g: (g, 0, 0)),
        compiler_params=_cparams(("arbitrary",)),
        name="compress_tokens",
    )(x, pos, w1_l, w2_l)
    kc = jnp.concatenate([out[0], out[1]], axis=1)
    vct = out[2:4].reshape(2, n_chunk // BLOCK, BLOCK, HEAD_DIM).transpose(0, 1, 3, 2)
    return kc, vct


def _cmp_attn_kernel(qt_ref, kc_ref, vct_ref, mt_ref, o_ref, mq_ref, e_s, impsel_s, *, n_cmp, n_blk):
    qi = pl.program_id(0)
    row_half = lax.broadcasted_iota(jnp.int32, (LANES, BLOCK), 0) >> 6
    qt_pairs = []
    for kvh in range(2):
        qts = []
        for h in range(4):
            r0 = (kvh * 4 + h) * HEAD_DIM
            qh = qt_ref[r0:r0 + HEAD_DIM, :].astype(F32)
            both = jnp.concatenate([qh, qh], axis=0)
            qts.append(jnp.where(row_half == kvh, both, 0.0).astype(BF16))
        qt_pairs.append([jnp.concatenate(qts[0:2], axis=1), jnp.concatenate(qts[2:4], axis=1)])

    krow = lax.broadcasted_iota(jnp.int32, (BLOCK, 2 * BLOCK), 0)
    qcol = lax.broadcasted_iota(jnp.int32, (BLOCK, 2 * BLOCK), 1) & (BLOCK - 1)
    qpos = qi * BLOCK + qcol
    mtl = mt_ref[...]

    def attend(nc):
        for kvh in range(2):
            attend_head(nc, kvh)

    def attend_head(nc, kvh):
        m_locs, l_locs, pvs = [], [], []
        for c in range(nc):
            r0 = c * BLOCK
            kcc = kc_ref[r0:r0 + BLOCK, :]
            vcc = vct_ref[kvh, c]
            valid = (r0 + krow) * CMP_STRIDE + (CMP_BLOCK - 1) <= qpos
            ml, ll, pl_ = [], [], []
            for pair in range(2):
                cols = slice(pair * 2 * BLOCK, (pair + 1) * 2 * BLOCK)
                s = jnp.where(valid, _dot(kcc, qt_pairs[kvh][pair]), NEG)
                m_loc = jnp.max(s, axis=0, keepdims=True)
                e = jnp.where(valid, jnp.exp(s - m_loc), 0.0)
                e_s[kvh, r0:r0 + BLOCK, cols] = e
                ml.append(m_loc)
                ll.append(jnp.sum(e, axis=0, keepdims=True))
                pl_.append(_dot(vcc, e.astype(BF16)))
            m_locs.append(ml)
            l_locs.append(ll)
            pvs.append(pl_)
        scales = [[None, None] for _ in range(nc)]
        for pair in range(2):
            m_fin = m_locs[0][pair]
            for c in range(1, nc):
                m_fin = jnp.maximum(m_fin, m_locs[c][pair])
            ws = [jnp.exp(m_locs[c][pair] - m_fin) for c in range(nc)]
            den = ws[0] * l_locs[0][pair]
            acc = ws[0] * pvs[0][pair]
            for c in range(1, nc):
                den = den + ws[c] * l_locs[c][pair]
                acc = acc + ws[c] * pvs[c][pair]
            inv_l = 1.0 / jnp.maximum(den, 1e-30)
            o = acc * inv_l
            oc = (kvh * 2 + pair) * LANES
            o_ref[:, oc:oc + LANES] = jnp.concatenate([o[:, 0:BLOCK], o[:, BLOCK:2 * BLOCK]], axis=0).T
            for c in range(nc):
                scales[c][pair] = ws[c] * inv_l
        impsel_s[kvh] = jnp.zeros(impsel_s.shape[1:], F32)
        for c in range(nc):
            r0 = c * BLOCK
            pn = e_s[kvh, r0:r0 + BLOCK, :] * jnp.concatenate(scales[c], axis=1)
            imp = pn[:, 0:BLOCK] + pn[:, BLOCK:2 * BLOCK] + pn[:, 2 * BLOCK:3 * BLOCK] + pn[:, 3 * BLOCK:4 * BLOCK]
            hi = imp.astype(BF16)
            r1 = imp - hi.astype(F32)
            mid = r1.astype(BF16)
            lo = (r1 - mid.astype(F32)).astype(BF16)
            b0 = c * (BLOCK // 4)
            impsel_s[kvh, b0:b0 + 64, :] += _dot(mtl, hi) + _dot(mtl, mid) + _dot(mtl, lo)

    n_ch = n_cmp // BLOCK
    if n_ch >= 4:
        need = qi // CMP_STRIDE + 1
        levels = sorted({n_ch // 4, n_ch // 2, 3 * n_ch // 4, n_ch})
        for li, level in enumerate(levels):
            lo = levels[li - 1] if li else 0

            @pl.when((need > lo) & (need <= level))
            def _(level=level):
                attend(level)
    else:
        attend(n_ch)

    blk = lax.broadcasted_iota(jnp.int32, (n_blk, BLOCK), 0)
    cur = (qi * BLOCK + lax.broadcasted_iota(jnp.int32, (n_blk, BLOCK), 1)) >> 6
    forced = (blk == 0) | (blk == cur) | (blk == cur - 1)
    causal = blk <= cur
    blk_f = blk.astype(F32)
    scores = [jnp.where(forced, -jnp.inf, jnp.where(causal, impsel_s[kvh, 0:n_blk, :], NEG)) for kvh in range(2)]
    chosen = [jnp.where(forced | (cur < SEL_TOPK), 1.0, 0.0) for _ in range(2)]
    for _ in range(SEL_TOPK - 3):
        for kvh in range(2):
            mx = jnp.max(scores[kvh], axis=0, keepdims=True)
            first = jnp.min(jnp.where(scores[kvh] == mx, blk_f, 1e9), axis=0, keepdims=True)
            pick = blk_f == first
            chosen[kvh] = jnp.where(pick, 1.0, chosen[kvh])
            scores[kvh] = jnp.where(pick, -jnp.inf, scores[kvh])
    for kvh in range(2):
        mq_ref[kvh] = jnp.where(causal & (chosen[kvh] > 0.5), 0.0, NEG).astype(BF16)


def _importance_matrix():
    mt = np.zeros((64, BLOCK), np.float32)
    ratio = SEL_BLOCK // CMP_STRIDE
    for o in range(-(CMP_BLOCK // CMP_STRIDE - 1), ratio):
        w = max(0, min(SEL_BLOCK, o * CMP_STRIDE + CMP_BLOCK) - max(0, o * CMP_STRIDE)) / CMP_BLOCK
        for r in range(BLOCK // ratio + 1):
            k = r * ratio + o
            if 0 <= k < BLOCK:
                mt[r, k] = w
    return jnp.asarray(mt, dtype=BF16)


def _cmp_attn(qt, kc, vct, seq):
    n_cmp = seq // CMP_STRIDE
    n_blk = max(LANES, seq // SEL_BLOCK)
    n_ch = n_cmp // BLOCK
    mt = _importance_matrix()
    kern = functools.partial(_cmp_attn_kernel, n_cmp=n_cmp, n_blk=n_blk)
    return pl.pallas_call(
        kern,
        out_shape=(jax.ShapeDtypeStruct((seq, 512), F32),
                   jax.ShapeDtypeStruct((2, seq // BLOCK, n_blk, BLOCK), BF16)),
        grid=(seq // BLOCK,),
        in_specs=[
            pl.BlockSpec((8 * HEAD_DIM, BLOCK), lambda i: (0, i)),
            pl.BlockSpec((n_cmp, LANES), lambda i: (0, 0)),
            pl.BlockSpec((2, n_ch, HEAD_DIM, BLOCK), lambda i: (0, 0, 0, 0)),
            pl.BlockSpec((64, BLOCK), lambda i: (0, 0)),
        ],
        out_specs=(pl.BlockSpec((BLOCK, 512), lambda i: (i, 0)),
                   pl.BlockSpec((2, None, n_blk, BLOCK), lambda i: (0, i, 0, 0))),
        scratch_shapes=[pltpu.VMEM((2, n_cmp, 4 * BLOCK), F32),
                        pltpu.VMEM((2, n_blk + 64, BLOCK), F32)],
        compiler_params=_cparams(("arbitrary",)),
        name="cmp_attn_topk",
    )(qt, kc, vct, mt)


def _sel_kernel(qi_ref, kt_ref, qt_ref, mq_ref, k_ref, vt_ref, e_ref, tab_ref, b31_ref, o_ref,
                qaug, m_s, acc_s, s_s, *, n_e):
    kvh = pl.program_id(0)
    step = pl.program_id(1)
    qi = qi_ref[step]
    kt = kt_ref[step]
    q_per_sub = SEL_SUB // SEL_TQ
    win_steps = LANES * SEL_BLOCK // SEL_TK
    ncol = 4 * SEL_TQ

    @pl.when(kt == 0)
    def _():
        m_s[...] = jnp.full(m_s.shape, -3e38, F32)
        acc_s[...] = jnp.zeros(acc_s.shape, F32)

    @pl.when(kt % win_steps == 0)
    def _():
        row_half = lax.broadcasted_iota(jnp.int32, (LANES, SEL_TQ), 0) >> 6
        mq = mq_ref[...]
        for h in range(4):
            qh = qt_ref[h * HEAD_DIM:(h + 1) * HEAD_DIM, :].astype(F32)
            both = jnp.concatenate([qh, qh], axis=0)
            qaug[0:LANES, h * SEL_TQ:(h + 1) * SEL_TQ] = jnp.where(row_half == kvh, both, 0.0).astype(BF16)
            qaug[LANES:2 * LANES, h * SEL_TQ:(h + 1) * SEL_TQ] = mq

    sub_near0 = jnp.maximum(0, (qi - SEL_SLAB_HI + q_per_sub - 1) // q_per_sub)

    def run(with_bias, n_sub):
        pcols = [slice(pair * 2 * SEL_TQ, (pair + 1) * 2 * SEL_TQ) for pair in range(2)]
        qas = [qaug[:, cols] for cols in pcols]
        m_runs = [m_s[:, cols] for cols in pcols]
        if with_bias:
            b31s = [jnp.concatenate([jnp.full((1, SEL_TQ), b31_ref[kvh * 4 + pair * 2 + hh], F32)
                                     for hh in range(2)], axis=1) for pair in range(2)]
            m_runs = [m + jnp.where(sub_near0 >= kt * SEL_NSUB, b, 0.0) for m, b in zip(m_runs, b31s)]
        colmax = [jnp.full((8, 2 * SEL_TQ), -3e38, F32) for _ in range(2)]
        n_chunk = n_sub * SEL_SUB // SEL_CHUNK

        def score_chunk(pair, idx):
            r0 = idx * SEL_CHUNK
            j, c = divmod(idx, SEL_SUB // SEL_CHUNK)
            sub = kt * SEL_NSUB + j
            is_far = sub < sub_near0
            kaug = jnp.concatenate([k_ref[r0:r0 + SEL_CHUNK, :],
                                    e_ref[sub % n_e, c * SEL_CHUNK:(c + 1) * SEL_CHUNK, :]], axis=1)
            s = _dot(kaug, qas[pair])
            if with_bias:
                far_shift = jnp.where(is_far, b31s[pair], 0.0)
                blocks = []
                for hh in range(2):
                    col = []
                    for tt in range(SEL_CHUNK // BLOCK):
                        t = c * (SEL_CHUNK // BLOCK) + tt
                        u = jnp.maximum(qi - q_per_sub * sub - t, SEL_SLAB_LO)
                        uidx = jnp.where(is_far, SEL_NSLAB - 1, u - SEL_SLAB_LO)
                        col.append(s[tt * BLOCK:(tt + 1) * BLOCK, hh * SEL_TQ:(hh + 1) * SEL_TQ]
                                   + tab_ref[pair * 2 + hh, uidx])
                    blocks.append(jnp.concatenate(col, axis=0))
                s = jnp.concatenate(blocks, axis=1) + far_shift
            s_s[r0:r0 + SEL_CHUNK, pcols[pair]] = s
            for g in range(SEL_CHUNK // 8):
                colmax[pair] = jnp.maximum(colmax[pair], s[g * 8:(g + 1) * 8, :])

        for idx in range(n_chunk):
            score_chunk(0, idx)
            score_chunk(1, idx)
        m_news = [jnp.maximum(m, jnp.max(cm, axis=0, keepdims=True)) for m, cm in zip(m_runs, colmax)]
        accs = [jnp.exp(m_runs[pair] - m_news[pair]) * acc_s[:, pcols[pair]] for pair in range(2)]
        for idx in range(n_chunk):
            r0 = idx * SEL_CHUNK
            vt = vt_ref[0:SEL_VROWS, r0:r0 + SEL_CHUNK]
            for pair in range(2):
                pr = jnp.exp(s_s[r0:r0 + SEL_CHUNK, pcols[pair]] - m_news[pair]).astype(BF16)
                accs[pair] = accs[pair] + _dot(vt, pr)
        for pair in range(2):
            acc_s[:, pcols[pair]] = accs[pair]
            m_s[:, pcols[pair]] = m_news[pair]

    all_far = kt * SEL_NSUB + SEL_NSUB - 1 < sub_near0
    n_live = jnp.minimum(SEL_NSUB, qi // q_per_sub - kt * SEL_NSUB + 1)

    @pl.when(all_far)
    def _():
        run(False, SEL_NSUB)

    for live in range(1, SEL_NSUB + 1):
        @pl.when(jnp.logical_not(all_far) & (n_live == live))
        def _(live=live):
            run(True, live)

    @pl.when(kt == qi // (SEL_TK // SEL_TQ))
    def _():
        acc = acc_s[...]
        o = acc[0:HEAD_DIM, :] / acc[HEAD_DIM:HEAD_DIM + 1, :]
        for jp in range(2):
            blk = jnp.concatenate([o[:, (2 * jp) * SEL_TQ:(2 * jp + 1) * SEL_TQ],
                                   o[:, (2 * jp + 1) * SEL_TQ:(2 * jp + 2) * SEL_TQ]], axis=0)
            o_ref[:, jp * LANES:(jp + 1) * LANES] = blk.T


def _sel_attn(qkv, qt, vt, mq, bias_s, rel_bias, seq):
    n_q = seq // SEL_TQ
    per_step = SEL_TK // SEL_TQ
    qi_l, kt_l = [], []
    for qi in range(n_q):
        for kt in range(qi // per_step + 1):
            qi_l.append(qi)
            kt_l.append(kt)
    qi_arr = jnp.asarray(np.array(qi_l, np.int32))
    kt_arr = jnp.asarray(np.array(kt_l, np.int32))
    win_steps = LANES * SEL_BLOCK // SEL_TK
    n_e = min(LANES * SEL_BLOCK // SEL_SUB, seq // SEL_SUB)
    e_np = np.zeros((n_e, SEL_SUB, LANES), np.float32)
    for t in range(n_e):
        for k in range(SEL_SUB):
            e_np[t, k, (t * (SEL_SUB // SEL_BLOCK) + k // SEL_BLOCK) % LANES] = 1.0
    e_all = jnp.asarray(e_np, dtype=BF16)
    b31 = rel_bias[REL_BUCKETS - 1, 4:12]
    grid_spec = pltpu.PrefetchScalarGridSpec(
        num_scalar_prefetch=2,
        grid=(2, len(qi_l)),
        in_specs=[
            pl.BlockSpec((4 * HEAD_DIM, SEL_TQ), lambda h, s, qi, kt: (h, qi[s])),
            pl.BlockSpec((None, None, LANES, SEL_TQ), lambda h, s, qi, kt: (h, qi[s], kt[s] // win_steps, 0)),
            pl.BlockSpec((SEL_TK, LANES), lambda h, s, qi, kt: (kt[s], _qkv_col(COL_CKS) // LANES)),
            pl.BlockSpec((None, LANES, SEL_TK), lambda h, s, qi, kt: (h, 0, kt[s])),
            pl.BlockSpec((n_e, SEL_SUB, LANES), lambda h, s, qi, kt: (0, 0, 0)),
            pl.BlockSpec((4, SEL_NSLAB, BLOCK, BLOCK), lambda h, s, qi, kt: (h, 0, 0, 0)),
            pl.BlockSpec(memory_space=pltpu.SMEM),
        ],
        out_specs=pl.BlockSpec((SEL_TQ, 256), lambda h, s, qi, kt: (qi[s], h)),
        scratch_shapes=[pltpu.VMEM((2 * LANES, 4 * SEL_TQ), BF16),
                        pltpu.VMEM((1, 4 * SEL_TQ), F32),
                        pltpu.VMEM((SEL_VROWS, 4 * SEL_TQ), F32),
                        pltpu.VMEM((SEL_TK, 4 * SEL_TQ), F32)],
    )
    return pl.pallas_call(
        functools.partial(_sel_kernel, n_e=n_e),
        out_shape=jax.ShapeDtypeStruct((seq, 512), F32),
        grid_spec=grid_spec,
        compiler_params=_cparams(("arbitrary", "arbitrary")),
        name="selected_attn",
    )(qi_arr, kt_arr, qt, mq, qkv, vt, e_all, bias_s, b31)


def _layer_norm(z, g, b):
    mu = jnp.mean(z, axis=-1, keepdims=True)
    zc = z - mu
    var = jnp.mean(jnp.square(zc), axis=-1, keepdims=True)
    return zc * lax.rsqrt(var + LN_EPS) * g + b


def _outproj_kernel(x_ref, oa0, oa1, oa2, la0, la1, la2, ob_ref, ocmp_ref, oslc_ref, owin_ref, g_ref,
                    gx_ref, w_ref, ga_ref, lng_ref, lnb_ref, o_ref):
    l0, l1, l2 = la0[...], la1[...], la2[...]
    mx = jnp.maximum(jnp.maximum(l0, l1), l2)
    w0, w1, w2 = jnp.exp(l0 - mx), jnp.exp(l1 - mx), jnp.exp(l2 - mx)
    oa = (w0 * oa0[...] + w1 * oa1[...] + w2 * oa2[...]) / (w0 + w1 + w2)
    gt = jax.nn.sigmoid(g_ref[...])
    hi = gt.astype(BF16)
    lo = (gt - hi.astype(F32)).astype(BF16)
    gx = gx_ref[...]
    gates = _dot(hi, gx) + _dot(lo, gx)
    oc = gates[:, 0:512] * ocmp_ref[...] + gates[:, 512:1024] * oslc_ref[...] + gates[:, 1024:1536] * owin_ref[...]
    mixed = jnp.concatenate([oa, ob_ref[...], oc], axis=1).astype(BF16)
    y = _dot(mixed, w_ref[...])
    z = DN_ALPHA * x_ref[...] + (1 + ga_ref[...]) * y
    o_ref[...] = _layer_norm(z, lng_ref[...], lnb_ref[...])


def _gate_expand():
    gx = np.zeros((LANES, 3 * 512), np.float32)
    for h in range(8):
        for b in range(3):
            gx[h * 3 + b, b * 512 + h * HEAD_DIM:b * 512 + (h + 1) * HEAD_DIM] = 1.0
    return jnp.asarray(gx, dtype=BF16)


def _outproj(x, oa, la, ob, ocmp, oslc, owin, gates, w_out_l, mod_l, lng, lnb, seq):
    tm = 256
    row = lambda w: pl.BlockSpec((tm, w), lambda i: (i, 0))
    vec = lambda k: pl.BlockSpec((1, D_MODEL), lambda i, k=k: (0, k))
    return pl.pallas_call(
        _outproj_kernel,
        out_shape=jax.ShapeDtypeStruct((seq, D_MODEL), F32),
        grid=(seq // tm,),
        in_specs=[row(D_MODEL)] + [row(256)] * 7 + [row(512)] * 3 + [row(LANES)] + [
            pl.BlockSpec((LANES, 3 * 512), lambda i: (0, 0)),
            pl.BlockSpec((D_MODEL, D_MODEL), lambda i: (0, 0)),
            vec(2), vec(0), vec(0)],
        out_specs=row(D_MODEL),
        compiler_params=_cparams(("arbitrary",)),
        name="out_proj_ln",
    )(x, oa[0], oa[1], oa[2], la[0], la[1], la[2], ob, ocmp, oslc, owin, gates, _gate_expand(), w_out_l,
      mod_l, lng, lnb)


def _mlp_kernel(x_ref, sc_ref, sh_ref, g_ref, w1_ref, w2_ref, lng_ref, lnb_ref, o_ref, h_s, acc_s):
    j = pl.program_id(1)

    @pl.when(j == 0)
    def _():
        h_s[...] = (x_ref[...] * (1 + sc_ref[...]) + sh_ref[...]).astype(BF16)
        acc_s[...] = jnp.zeros(acc_s.shape, F32)

    f = jnp.maximum(_dot(h_s[...], w1_ref[...]), 0.0)
    acc_s[...] += _dot(jnp.square(f).astype(BF16), w2_ref[...])

    @pl.when(j == pl.num_programs(1) - 1)
    def _():
        z = DN_ALPHA * x_ref[...] + (1 + g_ref[...]) * acc_s[...]
        o_ref[...] = _layer_norm(z, lng_ref[...], lnb_ref[...])


def _mlp(x, mod_l, w1_l, w2_l, lng, lnb, seq):
    tm = 1024
    tf = 1024
    vec = lambda k: pl.BlockSpec((1, D_MODEL), lambda i, j, k=k: (0, k))
    return pl.pallas_call(
        _mlp_kernel,
        out_shape=jax.ShapeDtypeStruct((seq, D_MODEL), F32),
        grid=(seq // tm, D_FF // tf),
        in_specs=[
            pl.BlockSpec((tm, D_MODEL), lambda i, j: (i, 0)),
            vec(4), vec(3), vec(5),
            pl.BlockSpec((D_MODEL, tf), lambda i, j: (0, j)),
            pl.BlockSpec((tf, D_MODEL), lambda i, j: (j, 0)),
            vec(0), vec(0),
        ],
        out_specs=pl.BlockSpec((tm, D_MODEL), lambda i, j: (i, 0)),
        scratch_shapes=[pltpu.VMEM((tm, D_MODEL), BF16), pltpu.VMEM((tm, D_MODEL), F32)],
        compiler_params=_cparams(("arbitrary", "arbitrary")),
        name="mlp_ln",
    )(x, mod_l, mod_l, mod_l, w1_l, w2_l, lng, lnb)


def kernel(x, c, positions, w_in, w_out, rel_bias, sinks, cmp_pos, cmp_w1, cmp_w2, mlp_w1, mlp_w2,
           ada_w, ada_b, ln_g, ln_b):
    bsz, seq, d = x.shape
    assert bsz == 1 and d == D_MODEL and seq % (2 * A_CONFIGS[-1][0]) == 0
    x = x.reshape(seq, D_MODEL)

    mod = _modulation(c, ada_w, ada_b)
    cos_t, sin_t = _rope_tables(positions, seq)
    bias_a, bias_w, bias_s = _bias_tables(rel_bias)

    w_in_b = jnp.pad(w_in, ((0, 0), (0, 0), (0, N_IN_PAD - N_IN))).astype(BF16)
    w_out_b = w_out.astype(BF16)
    w1_b = mlp_w1.astype(BF16)
    w2_b = mlp_w2.astype(BF16)
    cw1_b = cmp_w1.astype(BF16)
    cw2_b = cmp_w2.astype(BF16)

    for l in range(DEPTH):
        mod_l = mod[l]
        qkv, ck, gates, qt, vt, a_qkv = _inproj(x, mod_l, w_in_b[l], cos_t, sin_t, seq)

        oa, la = [], []
        for ci, (window, dil) in enumerate(A_CONFIGS):
            nrows = seq // dil
            view = a_qkv.reshape(nrows, dil * N_A)
            o_c, l_c = _banded_call(
                view, nrows, dil, N_A // LANES, 0, 2, 4, 2,
                nprev=1, tq=min(BAND_TQ, nrows), max_dist=window // dil, bias=bias_a[ci], want_lse=True,
                group_div=None, kv_per_qblock=True, name=f"dilated_attn_{dil}")
            oa.append(o_c.reshape(seq, 256))
            la.append(l_c.reshape(seq, 256))

        (ob,) = _banded_call(
            qkv, seq, 1, N_QKV // LANES, _qkv_col(COL_BQ) // LANES, _qkv_col(COL_BK) // LANES,
            _qkv_col(COL_BV) // LANES, 2,
            nprev=1, tq=BAND_TQ, max_dist=B_WINDOW - 1, sinks=sinks[l], group_div=1,
            kv_per_qblock=False, name="swa_attn")

        kc, vct = _compress(ck, cmp_pos[l], cw1_b[l], cw2_b[l], seq)
        ocmp, mq = _cmp_attn(qt, kc, vct, seq)
        oslc = _sel_attn(qkv, qt, vt, mq, bias_s, rel_bias, seq)
        (owin,) = _banded_call(
            qkv, seq, 1, N_QKV // LANES, _qkv_col(COL_CQ) // LANES, _qkv_col(COL_CKW) // LANES,
            _qkv_col(COL_CVW) // LANES, 4,
            nprev=C_WINDOW // BLOCK, tq=BAND_TQ, max_dist=C_WINDOW - 1, bias=bias_w, group_div=2,
            kv_per_qblock=False, name="window_attn")

        x = _outproj(x, oa, la, ob, ocmp, oslc, owin, gates, w_out_b[l], mod_l,
                     ln_g[l, 0:1], ln_b[l, 0:1], seq)
        x = _mlp(x, mod_l, w1_b[l], w2_b[l], ln_g[l, 1:2], ln_b[l, 1:2], seq)

    return x.reshape(bsz, seq, D_MODEL)
```

```python
import functools
import math

import numpy as np
import jax
import jax.numpy as jnp
from jax import lax
from jax.experimental import pallas as pl
from jax.experimental.pallas import tpu as pltpu

F32 = jnp.float32
BF16 = jnp.bfloat16

D_MODEL = 1024
DEPTH = 4
HEAD_DIM = 64
LANES = 128
BLOCK = 128
A_CONFIGS = ((128, 1), (512, 4), (2048, 16))
B_WINDOW = 128
ROPE_THETA = 150000.0
CMP_BLOCK = 32
CMP_STRIDE = 16
CMP_HIDDEN = 256
SEL_BLOCK = 64
SEL_TOPK = 16
C_WINDOW = 512
REL_BUCKETS = 32
REL_MAX_DIST = 2048
D_FF = 4 * D_MODEL
DN_ALPHA = (2 * DEPTH) ** 0.25
LN_EPS = 1e-5
NEG = -1e30
FORCE = 1e4
SCALE = HEAD_DIM ** -0.5

COL_AQ, COL_AK, COL_AV = 0, 256, 512
COL_BQ, COL_BK, COL_BV = 768, 1024, 1152
COL_CQ = 1280
COL_CKC = 1792
COL_CKS, COL_CVS = 2048, 2176
COL_CKW, COL_CVW = 2304, 2432
COL_CG = 2560
N_A = 768
N_QKV = 1536


def _qkv_col(col):
    assert COL_BQ <= col < COL_CKC or COL_CKS <= col < COL_CG
    return col - N_A if col < COL_CKC else col - N_A - 256
N_IN = 2584
N_IN_PAD = 2688

BAND_TQ = 1024
SEL_TQ = 128
SEL_SUB = 512
SEL_CHUNK = 128
SEL_VROWS = 80
SEL_NSUB = 4
SEL_TK = SEL_SUB * SEL_NSUB
SEL_SLAB_LO = -1
SEL_SLAB_HI = 15
SEL_NSLAB = SEL_SLAB_HI - SEL_SLAB_LO + 2
VMEM_LIMIT = 56 * 1024 * 1024


def _cparams(sem, flags=None):
    return pltpu.CompilerParams(dimension_semantics=sem, vmem_limit_bytes=VMEM_LIMIT, flags=flags)


def _dot_nt(a, b):
    return lax.dot_general(a, b, (((1,), (1,)), ((), ())), preferred_element_type=F32)


def _dot(a, b):
    return jnp.dot(a, b, preferred_element_type=F32)


def _mod_kernel(c_ref, w_ref, b_ref, o_ref):
    c = c_ref[...]
    act = c * jax.nn.sigmoid(c)
    lhs = jnp.broadcast_to(act, (8, D_MODEL)).astype(BF16)
    y = _dot(lhs, w_ref[...].astype(BF16))
    o_ref[...] = y[0:1, :] + b_ref[...]


def _modulation(c, ada_w, ada_b):
    tn = 1536
    return pl.pallas_call(
        _mod_kernel,
        out_shape=jax.ShapeDtypeStruct((DEPTH, 1, 6 * D_MODEL), F32),
        grid=(DEPTH, 6 * D_MODEL // tn),
        in_specs=[
            pl.BlockSpec((1, D_MODEL), lambda l, j: (0, 0)),
            pl.BlockSpec((None, D_MODEL, tn), lambda l, j: (l, 0, j)),
            pl.BlockSpec((None, 1, tn), lambda l, j: (l, 0, j)),
        ],
        out_specs=pl.BlockSpec((None, 1, tn), lambda l, j: (l, 0, j)),
        compiler_params=_cparams(("arbitrary", "arbitrary")),
        name="adaln_mod",
    )(c, ada_w, ada_b.reshape(DEPTH, 1, 6 * D_MODEL))


def _rope_table_kernel(pos_ref, freq_ref, cos_ref, sin_ref):
    ang = pos_ref[...].astype(F32) * freq_ref[...]
    cos_ref[...] = jnp.cos(ang)
    sin_ref[...] = jnp.sin(ang)


def _rope_tables(positions, seq):
    half = HEAD_DIM // 2
    freq = ROPE_THETA ** (-jnp.arange(half, dtype=F32) / half)
    freq = jnp.tile(freq, LANES // half).reshape(1, LANES)
    tm = 1024
    return pl.pallas_call(
        _rope_table_kernel,
        out_shape=(jax.ShapeDtypeStruct((seq, LANES), F32),) * 2,
        grid=(seq // tm,),
        in_specs=[pl.BlockSpec((tm, 1), lambda i: (i, 0)), pl.BlockSpec((1, LANES), lambda i: (0, 0))],
        out_specs=(pl.BlockSpec((tm, LANES), lambda i: (i, 0)),) * 2,
        compiler_params=_cparams(("arbitrary",)),
        name="rope_tables",
    )(positions.reshape(seq, 1), freq)


def _t5_bias(rb_ref, dist, col, max_dist):
    exact = REL_BUCKETS // 2
    d = jnp.maximum(dist, 0)
    df = jnp.maximum(d, 1).astype(F32)
    large = exact + (jnp.log(df / exact) / math.log(REL_MAX_DIST / exact)
                     * (REL_BUCKETS - exact)).astype(jnp.int32)
    large = jnp.minimum(large, REL_BUCKETS - 1)
    bucket = jnp.where(d < exact, d, large)
    out = jnp.zeros(dist.shape, F32)
    for b in range(REL_BUCKETS):
        out = jnp.where(bucket == b, rb_ref[b * 12 + col], out)
    return out, bucket


def _bias_a_kernel(rb_ref, o_ref):
    cfg = pl.program_id(0)
    head = pl.program_id(1)
    dil = jnp.where(cfg == 0, 1, jnp.where(cfg == 1, 4, 16))
    r = lax.broadcasted_iota(jnp.int32, (BLOCK, 2 * BLOCK), 0)
    c = lax.broadcasted_iota(jnp.int32, (BLOCK, 2 * BLOCK), 1)
    dist = r + BLOCK - c
    bias, _ = _t5_bias(rb_ref, dist * dil, head, None)
    o_ref[...] = jnp.where((dist >= 0) & (dist <= BLOCK), bias, NEG)


def _bias_w_kernel(rb_ref, o_ref):
    head = pl.program_id(0)
    nprev = C_WINDOW // BLOCK
    r = lax.broadcasted_iota(jnp.int32, (BLOCK, (nprev + 1) * BLOCK), 0)
    c = lax.broadcasted_iota(jnp.int32, (BLOCK, (nprev + 1) * BLOCK), 1)
    dist = r + nprev * BLOCK - c
    bias, _ = _t5_bias(rb_ref, dist, 4 + head, None)
    o_ref[...] = jnp.where((dist >= 0) & (dist <= C_WINDOW - 1), bias, NEG)


def _bias_s_kernel(rb_ref, o_ref):
    head = pl.program_id(0)
    slab = pl.program_id(1)
    r = lax.broadcasted_iota(jnp.int32, (BLOCK, BLOCK), 1)
    c = lax.broadcasted_iota(jnp.int32, (BLOCK, BLOCK), 0)
    dist = (slab + SEL_SLAB_LO) * BLOCK + r - c
    bias, _ = _t5_bias(rb_ref, dist, 4 + head, None)
    val = jnp.where(dist >= 0, bias, NEG)
    o_ref[...] = jnp.where(slab == SEL_NSLAB - 1, 0.0, val)


def _bias_tables(rel_bias):
    rb = rel_bias.reshape(-1)
    smem = pl.BlockSpec(memory_space=pltpu.SMEM)
    bias_a = pl.pallas_call(
        _bias_a_kernel,
        out_shape=jax.ShapeDtypeStruct((3, 4, BLOCK, 2 * BLOCK), F32),
        grid=(3, 4),
        in_specs=[smem],
        out_specs=pl.BlockSpec((None, None, BLOCK, 2 * BLOCK), lambda a, h: (a, h, 0, 0)),
        compiler_params=_cparams(("arbitrary", "arbitrary")),
        name="bias_table_dilated",
    )(rb)
    wctx = C_WINDOW + BLOCK
    bias_w = pl.pallas_call(
        _bias_w_kernel,
        out_shape=jax.ShapeDtypeStruct((8, BLOCK, wctx), F32),
        grid=(8,),
        in_specs=[smem],
        out_specs=pl.BlockSpec((None, BLOCK, wctx), lambda h: (h, 0, 0)),
        compiler_params=_cparams(("arbitrary",)),
        name="bias_table_window",
    )(rb)
    bias_s = pl.pallas_call(
        _bias_s_kernel,
        out_shape=jax.ShapeDtypeStruct((8, SEL_NSLAB, BLOCK, BLOCK), F32),
        grid=(8, SEL_NSLAB),
        in_specs=[smem],
        out_specs=pl.BlockSpec((None, None, BLOCK, BLOCK), lambda h, u: (h, u, 0, 0)),
        compiler_params=_cparams(("arbitrary", "arbitrary")),
        name="bias_table_selected",
    )(rb)
    return bias_a, bias_w, bias_s


def _rope_apply(t, cos, sin, first):
    up = pltpu.roll(t, LANES - HEAD_DIM // 2, axis=1)
    dn = pltpu.roll(t, HEAD_DIM // 2, axis=1)
    return t * cos + jnp.where(first, -up, dn) * sin


def _inproj_kernel(x_ref, sc_ref, sh_ref, w_ref, cos_ref, sin_ref, qkv_ref, ck_ref, g_ref, qt_ref, vt_ref,
                   a_ref):
    h = (x_ref[...] * (1 + sc_ref[...]) + sh_ref[...]).astype(BF16)
    y = _dot(h, w_ref[...])
    cos = cos_ref[...]
    sin = sin_ref[...]
    lane = lax.broadcasted_iota(jnp.int32, cos.shape, 1)
    first = (lane & (HEAD_DIM - 1)) < HEAD_DIM // 2

    def put(c0, width, val):
        q0 = _qkv_col(c0)
        qkv_ref[:, q0:q0 + width] = val.astype(BF16)

    a_ref[:, 0:256] = (y[:, COL_AQ:COL_AQ + 256] * SCALE).astype(BF16)
    a_ref[:, 256:768] = y[:, COL_AK:COL_AK + 512].astype(BF16)
    for j in range(2):
        c0 = COL_BQ + j * LANES
        put(c0, LANES, _rope_apply(y[:, c0:c0 + LANES], cos, sin, first) * SCALE)
    put(COL_BK, LANES, _rope_apply(y[:, COL_BK:COL_BK + LANES], cos, sin, first))
    put(COL_BV, LANES, y[:, COL_BV:COL_BV + LANES])
    put(COL_CQ, 512, y[:, COL_CQ:COL_CQ + 512] * SCALE)
    put(COL_CKS, 512, y[:, COL_CKS:COL_CKS + 512])
    for g in range(4):
        c0 = COL_CKC + g * HEAD_DIM
        ck_ref[g] = y[:, c0:c0 + HEAD_DIM]
    g_ref[...] = y[:, COL_CG:COL_CG + LANES]
    qt_ref[...] = (y[:, COL_CQ:COL_CQ + 512] * SCALE).T.astype(BF16)
    vt = y[:, COL_CVS:COL_CVS + LANES].T
    ones_row = (lax.broadcasted_iota(jnp.int32, (HEAD_DIM, vt.shape[1]), 0) == 0).astype(F32)
    for kvh in range(2):
        vt_ref[kvh] = jnp.concatenate([vt[kvh * HEAD_DIM:(kvh + 1) * HEAD_DIM, :], ones_row], axis=0).astype(BF16)


def _inproj(x, mod_l, w_in_l, cos_t, sin_t, seq):
    tm = 512
    vec = lambda k: pl.BlockSpec((1, D_MODEL), lambda i, k=k: (0, k))
    return pl.pallas_call(
        _inproj_kernel,
        out_shape=(jax.ShapeDtypeStruct((seq, N_QKV), BF16),
                   jax.ShapeDtypeStruct((4, seq, HEAD_DIM), F32),
                   jax.ShapeDtypeStruct((seq, LANES), F32),
                   jax.ShapeDtypeStruct((512, seq), BF16),
                   jax.ShapeDtypeStruct((2, LANES, seq), BF16),
                   jax.ShapeDtypeStruct((seq, N_A), BF16)),
        grid=(seq // tm,),
        in_specs=[
            pl.BlockSpec((tm, D_MODEL), lambda i: (i, 0)),
            vec(1), vec(0),
            pl.BlockSpec((D_MODEL, N_IN_PAD), lambda i: (0, 0)),
            pl.BlockSpec((tm, LANES), lambda i: (i, 0)),
            pl.BlockSpec((tm, LANES), lambda i: (i, 0)),
        ],
        out_specs=(pl.BlockSpec((tm, N_QKV), lambda i: (i, 0)),
                   pl.BlockSpec((4, tm, HEAD_DIM), lambda i: (0, i, 0)),
                   pl.BlockSpec((tm, LANES), lambda i: (i, 0)),
                   pl.BlockSpec((512, tm), lambda i: (0, i)),
                   pl.BlockSpec((2, LANES, tm), lambda i: (0, 0, i)),
                   pl.BlockSpec((tm, N_A), lambda i: (i, 0))),
        compiler_params=_cparams(("arbitrary",)),
        name="in_proj",
    )(x, mod_l, mod_l, w_in_l, cos_t, sin_t)


def _lane_half():
    return lax.broadcasted_iota(jnp.int32, (BLOCK, LANES), 1) >> 6


def _align_q(q, half, x, ysel):
    qx = jnp.where(half == x, q, 0.0)
    if ysel is None:
        return qx.astype(BF16)
    qd = qx + pltpu.roll(qx, HEAD_DIM, axis=1)
    return jnp.where(ysel, qd, 0.0).astype(BF16)


def _spread_o(o, ysel):
    if ysel is None:
        return o
    ob = jnp.where(ysel, o, 0.0)
    return ob + pltpu.roll(ob, HEAD_DIM, axis=1)


def _banded_kernel(*refs, nprev, tq, max_dist, use_bias, use_sinks, want_lse, group_div, row_axis):
    refs = list(refs)
    sink_ref = refs.pop(0) if use_sinks else None
    q_ref, kp_ref, kc_ref, vp_ref, vc_ref = refs[:5]
    refs = refs[5:]
    bias_ref = refs.pop(0) if use_bias else None
    o_ref = refs.pop(0)
    lse_ref = refs.pop(0) if want_lse else None
    kctx, vctx = refs

    tp = nprev * BLOCK
    ctx = (nprev + 1) * BLOCK
    i = pl.program_id(row_axis)
    p = pl.program_id(row_axis - 1)
    kctx[0:tp, :] = kp_ref[...]
    kctx[tp:tp + tq, :] = kc_ref[...]
    vctx[0:tp, :] = vp_ref[...]
    vctx[tp:tp + tq, :] = vc_ref[...]

    half = _lane_half()
    upper = half == 1
    ysel = None if group_div is None else half == p // group_div
    chunks = [(c0, min(c0 + 2 * BLOCK, ctx)) for c0 in range(0, ctx, 2 * BLOCK)]
    rrs, ccs, bands = [], [], []
    for c0, c1 in chunks:
        rr = lax.broadcasted_iota(jnp.int32, (BLOCK, c1 - c0), 0)
        cc = lax.broadcasted_iota(jnp.int32, (BLOCK, c1 - c0), 1) + c0
        ccs.append(cc)
        if not use_bias:
            dist = rr + tp - cc
            bands.append((dist >= 0) & (dist <= max_dist))

    for sub in range(tq // BLOCK):
        rows = slice(sub * BLOCK, (sub + 1) * BLOCK)
        q = q_ref[rows, :].astype(F32)
        low = tp - (i * tq + sub * BLOCK)
        outs, lses = [], []
        q2 = jnp.concatenate([_align_q(q, half, x, ysel) for x in range(2)], axis=0)
        stats = [([], [], []), ([], [], [])]
        for ci, (c0, c1) in enumerate(chunks):
            k = kctx[sub * BLOCK + c0:sub * BLOCK + c1, :]
            v = vctx[sub * BLOCK + c0:sub * BLOCK + c1, :]
            s2 = _dot_nt(q2, k)
            es = []
            for x in range(2):
                s = s2[x * BLOCK:(x + 1) * BLOCK, :]
                if use_bias:
                    s = s + bias_ref[x, :, c0:c1]
                else:
                    s = jnp.where(bands[ci], s, NEG)
                s = jnp.where(ccs[ci] >= low, s, NEG)
                m_loc = jnp.max(s, axis=1, keepdims=True)
                e = jnp.exp(s - m_loc)
                stats[x][0].append(m_loc)
                stats[x][1].append(jnp.sum(e, axis=1, keepdims=True))
                es.append(e.astype(BF16))
            pv2 = _dot(jnp.concatenate(es, axis=0), v)
            for x in range(2):
                stats[x][2].append(pv2[x * BLOCK:(x + 1) * BLOCK, :])
        for x in range(2):
            ms, dens, pvs = stats[x]
            m = ms[0]
            for m_loc in ms[1:]:
                m = jnp.maximum(m, m_loc)
            if use_sinks:
                sk = sink_ref[2 * p + x]
                m = jnp.maximum(m, sk)
            ws = [jnp.exp(m_loc - m) for m_loc in ms]
            den = ws[0] * dens[0]
            acc = ws[0] * pvs[0]
            for w, d, pv in zip(ws[1:], dens[1:], pvs[1:]):
                den = den + w * d
                acc = acc + w * pv
            if use_sinks:
                den = den + jnp.exp(sk - m)
            o = acc / den
            outs.append(_spread_o(o, ysel))
            lses.append(m + jnp.log(den))
        o_ref[rows, :] = jnp.where(upper, outs[1], outs[0])
        if want_lse:
            lse_ref[rows, :] = jnp.where(upper, jnp.broadcast_to(lses[1], (BLOCK, LANES)),
                                         jnp.broadcast_to(lses[0], (BLOCK, LANES)))


def _banded_call(qkv_view, nrows, nsub, row_width_blocks, q_cb, k_cb, v_cb, n_qblocks, *, nprev, tq,
                 max_dist, bias=None, sinks=None, want_lse=False, group_div=None, kv_per_qblock, name):
    tp = nprev * BLOCK
    ratio = tq // tp
    ctx = (nprev + 1) * BLOCK

    def kvb(p):
        return p if kv_per_qblock else 0

    in_specs = []
    args = []
    if sinks is not None:
        in_specs.append(pl.BlockSpec(memory_space=pltpu.SMEM))
        args.append(sinks)
    in_specs += [
        pl.BlockSpec((tq, LANES), lambda r, p, i: (i, r * row_width_blocks + q_cb + p)),
        pl.BlockSpec((tp, LANES), lambda r, p, i: (jnp.maximum(i * ratio - 1, 0), r * row_width_blocks + k_cb + kvb(p))),
        pl.BlockSpec((tq, LANES), lambda r, p, i: (i, r * row_width_blocks + k_cb + kvb(p))),
        pl.BlockSpec((tp, LANES), lambda r, p, i: (jnp.maximum(i * ratio - 1, 0), r * row_width_blocks + v_cb + kvb(p))),
        pl.BlockSpec((tq, LANES), lambda r, p, i: (i, r * row_width_blocks + v_cb + kvb(p))),
    ]
    args += [qkv_view] * 5
    if bias is not None:
        in_specs.append(pl.BlockSpec((2, BLOCK, ctx), lambda r, p, i: (p, 0, 0)))
        args.append(bias)
    out_w = nsub * n_qblocks * LANES
    o_spec = pl.BlockSpec((tq, LANES), lambda r, p, i: (i, r * n_qblocks + p))
    out_shape = [jax.ShapeDtypeStruct((nrows, out_w), F32)]
    out_specs = [o_spec]
    if want_lse:
        out_shape.append(jax.ShapeDtypeStruct((nrows, out_w), F32))
        out_specs.append(o_spec)
    kern = functools.partial(_banded_kernel, nprev=nprev, tq=tq, max_dist=max_dist, use_bias=bias is not None,
                             use_sinks=sinks is not None, want_lse=want_lse, group_div=group_div, row_axis=2)
    return pl.pallas_call(
        kern,
        out_shape=tuple(out_shape),
        grid=(nsub, n_qblocks, nrows // tq),
        in_specs=in_specs,
        out_specs=tuple(out_specs),
        scratch_shapes=[pltpu.VMEM((tp + tq, LANES), BF16), pltpu.VMEM((tp + tq, LANES), BF16)],
        compiler_params=_cparams(("arbitrary", "arbitrary", "arbitrary")),
        name=name,
    )(*args)


def _compress_kernel(x_ref, pos_ref, w1_ref, w2_ref, o_ref):
    x = x_ref[...]
    half = CMP_STRIDE * HEAD_DIM
    xa = (x + pos_ref[0:1, :]).astype(BF16)
    xb = (x + pos_ref[1:2, :]).astype(BF16)
    first = _dot(xa, w1_ref[0:half, :])
    second = _dot(xb, w1_ref[half:2 * half, :])
    n = x.shape[0]
    hid = first + pltpu.roll(second, n - 1, axis=0)
    act = jax.nn.gelu(hid)
    o_ref[...] = _dot(act.astype(BF16), w2_ref[...]).astype(BF16)


def _compress(ck, cmp_pos_l, w1_l, w2_l, seq):
    n_chunk = seq // CMP_STRIDE
    feat = CMP_STRIDE * HEAD_DIM
    x = ck.reshape(4, n_chunk, feat)
    pos = cmp_pos_l.reshape(2, 2, feat)
    out = pl.pallas_call(
        _compress_kernel,
        out_shape=jax.ShapeDtypeStruct((4, n_chunk, HEAD_DIM), BF16),
        grid=(4,),
        in_specs=[
            pl.BlockSpec((None, n_chunk, feat), lambda g: (g, 0, 0)),
            pl.BlockSpec((None, 2, feat), lambda g: (g // 2, 0, 0)),
            pl.BlockSpec((None, 2 * feat, CMP_HIDDEN), lambda g: (g // 2, 0, 0)),
            pl.BlockSpec((None, CMP_HIDDEN, HEAD_DIM), lambda g: (g // 2, 0, 0)),
        ],
        out_specs=pl.BlockSpec((None, n_chunk, HEAD_DIM), lambda g: (g, 0, 0)),
        compiler_params=_cparams(("arbitrary",)),
        name="compress_tokens",
    )(x, pos, w1_l, w2_l)
    kc = jnp.concatenate([out[0], out[1]], axis=1)
    vct = out[2:4].reshape(2, n_chunk // BLOCK, BLOCK, HEAD_DIM).transpose(0, 1, 3, 2)
    return kc, vct


def _cmp_attn_kernel(qt_ref, kc_ref, vct_ref, mt_ref, o_ref, mq_ref, e_s, impsel_s, *, n_cmp, n_blk):
    qi = pl.program_id(0)
    row_half = lax.broadcasted_iota(jnp.int32, (LANES, BLOCK), 0) >> 6
    qt_pairs = []
    for kvh in range(2):
        qts = []
        for h in range(4):
            r0 = (kvh * 4 + h) * HEAD_DIM
            qh = qt_ref[r0:r0 + HEAD_DIM, :].astype(F32)
            both = jnp.concatenate([qh, qh], axis=0)
            qts.append(jnp.where(row_half == kvh, both, 0.0).astype(BF16))
        qt_pairs.append([jnp.concatenate(qts[0:2], axis=1), jnp.concatenate(qts[2:4], axis=1)])

    krow = lax.broadcasted_iota(jnp.int32, (BLOCK, 2 * BLOCK), 0)
    qcol = lax.broadcasted_iota(jnp.int32, (BLOCK, 2 * BLOCK), 1) & (BLOCK - 1)
    qpos = qi * BLOCK + qcol
    mtl = mt_ref[...]

    def attend(nc):
        for kvh in range(2):
            attend_head(nc, kvh)

    def attend_head(nc, kvh):
        m_locs, l_locs, pvs = [], [], []
        for c in range(nc):
            r0 = c * BLOCK
            kcc = kc_ref[r0:r0 + BLOCK, :]
            vcc = vct_ref[kvh, c]
            valid = (r0 + krow) * CMP_STRIDE + (CMP_BLOCK - 1) <= qpos
            ml, ll, pl_ = [], [], []
            for pair in range(2):
                cols = slice(pair * 2 * BLOCK, (pair + 1) * 2 * BLOCK)
                s = jnp.where(valid, _dot(kcc, qt_pairs[kvh][pair]), NEG)
                m_loc = jnp.max(s, axis=0, keepdims=True)
                e = jnp.where(valid, jnp.exp(s - m_loc), 0.0)
                e_s[kvh, r0:r0 + BLOCK, cols] = e
                ml.append(m_loc)
                ll.append(jnp.sum(e, axis=0, keepdims=True))
                pl_.append(_dot(vcc, e.astype(BF16)))
            m_locs.append(ml)
            l_locs.append(ll)
            pvs.append(pl_)
        scales = [[None, None] for _ in range(nc)]
        for pair in range(2):
            m_fin = m_locs[0][pair]
            for c in range(1, nc):
                m_fin = jnp.maximum(m_fin, m_locs[c][pair])
            ws = [jnp.exp(m_locs[c][pair] - m_fin) for c in range(nc)]
            den = ws[0] * l_locs[0][pair]
            acc = ws[0] * pvs[0][pair]
            for c in range(1, nc):
                den = den + ws[c] * l_locs[c][pair]
                acc = acc + ws[c] * pvs[c][pair]
            inv_l = 1.0 / jnp.maximum(den, 1e-30)
            o = acc * inv_l
            oc = (kvh * 2 + pair) * LANES
            o_ref[:, oc:oc + LANES] = jnp.concatenate([o[:, 0:BLOCK], o[:, BLOCK:2 * BLOCK]], axis=0).T
            for c in range(nc):
                scales[c][pair] = ws[c] * inv_l
        impsel_s[kvh] = jnp.zeros(impsel_s.shape[1:], F32)
        for c in range(nc):
            r0 = c * BLOCK
            pn = e_s[kvh, r0:r0 + BLOCK, :] * jnp.concatenate(scales[c], axis=1)
            imp = pn[:, 0:BLOCK] + pn[:, BLOCK:2 * BLOCK] + pn[:, 2 * BLOCK:3 * BLOCK] + pn[:, 3 * BLOCK:4 * BLOCK]
            hi = imp.astype(BF16)
            r1 = imp - hi.astype(F32)
            mid = r1.astype(BF16)
            lo = (r1 - mid.astype(F32)).astype(BF16)
            b0 = c * (BLOCK // 4)
            impsel_s[kvh, b0:b0 + 64, :] += _dot(mtl, hi) + _dot(mtl, mid) + _dot(mtl, lo)

    n_ch = n_cmp // BLOCK
    if n_ch >= 4:
        need = qi // CMP_STRIDE + 1
        levels = sorted({n_ch // 4, n_ch // 2, 3 * n_ch // 4, n_ch})
        for li, level in enumerate(levels):
            lo = levels[li - 1] if li else 0

            @pl.when((need > lo) & (need <= level))
            def _(level=level):
                attend(level)
    else:
        attend(n_ch)

    blk = lax.broadcasted_iota(jnp.int32, (n_blk, BLOCK), 0)
    cur = (qi * BLOCK + lax.broadcasted_iota(jnp.int32, (n_blk, BLOCK), 1)) >> 6
    forced = (blk == 0) | (blk == cur) | (blk == cur - 1)
    causal = blk <= cur
    blk_f = blk.astype(F32)
    scores = [jnp.where(forced, -jnp.inf, jnp.where(causal, impsel_s[kvh, 0:n_blk, :], NEG)) for kvh in range(2)]
    chosen = [jnp.where(forced | (cur < SEL_TOPK), 1.0, 0.0) for _ in range(2)]
    for _ in range(SEL_TOPK - 3):
        for kvh in range(2):
            mx = jnp.max(scores[kvh], axis=0, keepdims=True)
            first = jnp.min(jnp.where(scores[kvh] == mx, blk_f, 1e9), axis=0, keepdims=True)
            pick = blk_f == first
            chosen[kvh] = jnp.where(pick, 1.0, chosen[kvh])
            scores[kvh] = jnp.where(pick, -jnp.inf, scores[kvh])
    for kvh in range(2):
        mq_ref[kvh] = jnp.where(causal & (chosen[kvh] > 0.5), 0.0, NEG).astype(BF16)


def _importance_matrix():
    mt = np.zeros((64, BLOCK), np.float32)
    ratio = SEL_BLOCK // CMP_STRIDE
    for o in range(-(CMP_BLOCK // CMP_STRIDE - 1), ratio):
        w = max(0, min(SEL_BLOCK, o * CMP_STRIDE + CMP_BLOCK) - max(0, o * CMP_STRIDE)) / CMP_BLOCK
        for r in range(BLOCK // ratio + 1):
            k = r * ratio + o
            if 0 <= k < BLOCK:
                mt[r, k] = w
    return jnp.asarray(mt, dtype=BF16)


def _cmp_attn(qt, kc, vct, seq):
    n_cmp = seq // CMP_STRIDE
    n_blk = max(LANES, seq // SEL_BLOCK)
    n_ch = n_cmp // BLOCK
    mt = _importance_matrix()
    kern = functools.partial(_cmp_attn_kernel, n_cmp=n_cmp, n_blk=n_blk)
    return pl.pallas_call(
        kern,
        out_shape=(jax.ShapeDtypeStruct((seq, 512), F32),
                   jax.ShapeDtypeStruct((2, seq // BLOCK, n_blk, BLOCK), BF16)),
        grid=(seq // BLOCK,),
        in_specs=[
            pl.BlockSpec((8 * HEAD_DIM, BLOCK), lambda i: (0, i)),
            pl.BlockSpec((n_cmp, LANES), lambda i: (0, 0)),
            pl.BlockSpec((2, n_ch, HEAD_DIM, BLOCK), lambda i: (0, 0, 0, 0)),
            pl.BlockSpec((64, BLOCK), lambda i: (0, 0)),
        ],
        out_specs=(pl.BlockSpec((BLOCK, 512), lambda i: (i, 0)),
                   pl.BlockSpec((2, None, n_blk, BLOCK), lambda i: (0, i, 0, 0))),
        scratch_shapes=[pltpu.VMEM((2, n_cmp, 4 * BLOCK), F32),
                        pltpu.VMEM((2, n_blk + 64, BLOCK), F32)],
        compiler_params=_cparams(("arbitrary",)),
        name="cmp_attn_topk",
    )(qt, kc, vct, mt)


def _sel_kernel(qi_ref, kt_ref, qt_ref, mq_ref, k_ref, vt_ref, e_ref, tab_ref, b31_ref, o_ref,
                qaug, m_s, acc_s, s_s, *, n_e):
    kvh = pl.program_id(0)
    step = pl.program_id(1)
    qi = qi_ref[step]
    kt = kt_ref[step]
    q_per_sub = SEL_SUB // SEL_TQ
    win_steps = LANES * SEL_BLOCK // SEL_TK
    ncol = 4 * SEL_TQ

    @pl.when(kt == 0)
    def _():
        m_s[...] = jnp.full(m_s.shape, -3e38, F32)
        acc_s[...] = jnp.zeros(acc_s.shape, F32)

    @pl.when(kt % win_steps == 0)
    def _():
        row_half = lax.broadcasted_iota(jnp.int32, (LANES, SEL_TQ), 0) >> 6
        mq = mq_ref[...]
        for h in range(4):
            qh = qt_ref[h * HEAD_DIM:(h + 1) * HEAD_DIM, :].astype(F32)
            both = jnp.concatenate([qh, qh], axis=0)
            qaug[0:LANES, h * SEL_TQ:(h + 1) * SEL_TQ] = jnp.where(row_half == kvh, both, 0.0).astype(BF16)
            qaug[LANES:2 * LANES, h * SEL_TQ:(h + 1) * SEL_TQ] = mq

    sub_near0 = jnp.maximum(0, (qi - SEL_SLAB_HI + q_per_sub - 1) // q_per_sub)

    def run(with_bias, n_sub):
        pcols = [slice(pair * 2 * SEL_TQ, (pair + 1) * 2 * SEL_TQ) for pair in range(2)]
        qas = [qaug[:, cols] for cols in pcols]
        m_runs = [m_s[:, cols] for cols in pcols]
        if with_bias:
            b31s = [jnp.concatenate([jnp.full((1, SEL_TQ), b31_ref[kvh * 4 + pair * 2 + hh], F32)
                                     for hh in range(2)], axis=1) for pair in range(2)]
            m_runs = [m + jnp.where(sub_near0 >= kt * SEL_NSUB, b, 0.0) for m, b in zip(m_runs, b31s)]
        colmax = [jnp.full((8, 2 * SEL_TQ), -3e38, F32) for _ in range(2)]
        n_chunk = n_sub * SEL_SUB // SEL_CHUNK

        def score_chunk(pair, idx):
            r0 = idx * SEL_CHUNK
            j, c = divmod(idx, SEL_SUB // SEL_CHUNK)
            sub = kt * SEL_NSUB + j
            is_far = sub < sub_near0
            kaug = jnp.concatenate([k_ref[r0:r0 + SEL_CHUNK, :],
                                    e_ref[sub % n_e, c * SEL_CHUNK:(c + 1) * SEL_CHUNK, :]], axis=1)
            s = _dot(kaug, qas[pair])
            if with_bias:
                far_shift = jnp.where(is_far, b31s[pair], 0.0)
                blocks = []
                for hh in range(2):
                    col = []
                    for tt in range(SEL_CHUNK // BLOCK):
                        t = c * (SEL_CHUNK // BLOCK) + tt
                        u = jnp.maximum(qi - q_per_sub * sub - t, SEL_SLAB_LO)
                        uidx = jnp.where(is_far, SEL_NSLAB - 1, u - SEL_SLAB_LO)
                        col.append(s[tt * BLOCK:(tt + 1) * BLOCK, hh * SEL_TQ:(hh + 1) * SEL_TQ]
                                   + tab_ref[pair * 2 + hh, uidx])
                    blocks.append(jnp.concatenate(col, axis=0))
                s = jnp.concatenate(blocks, axis=1) + far_shift
            s_s[r0:r0 + SEL_CHUNK, pcols[pair]] = s
            for g in range(SEL_CHUNK // 8):
                colmax[pair] = jnp.maximum(colmax[pair], s[g * 8:(g + 1) * 8, :])

        for idx in range(n_chunk):
            score_chunk(0, idx)
            score_chunk(1, idx)
        m_news = [jnp.maximum(m, jnp.max(cm, axis=0, keepdims=True)) for m, cm in zip(m_runs, colmax)]
        accs = [jnp.exp(m_runs[pair] - m_news[pair]) * acc_s[:, pcols[pair]] for pair in range(2)]
        for idx in range(n_chunk):
            r0 = idx * SEL_CHUNK
            vt = vt_ref[0:SEL_VROWS, r0:r0 + SEL_CHUNK]
            for pair in range(2):
                pr = jnp.exp(s_s[r0:r0 + SEL_CHUNK, pcols[pair]] - m_news[pair]).astype(BF16)
                accs[pair] = accs[pair] + _dot(vt, pr)
        for pair in range(2):
            acc_s[:, pcols[pair]] = accs[pair]
            m_s[:, pcols[pair]] = m_news[pair]

    all_far = kt * SEL_NSUB + SEL_NSUB - 1 < sub_near0
    n_live = jnp.minimum(SEL_NSUB, qi // q_per_sub - kt * SEL_NSUB + 1)

    @pl.when(all_far)
    def _():
        run(False, SEL_NSUB)

    for live in range(1, SEL_NSUB + 1):
        @pl.when(jnp.logical_not(all_far) & (n_live == live))
        def _(live=live):
            run(True, live)

    @pl.when(kt == qi // (SEL_TK // SEL_TQ))
    def _():
        acc = acc_s[...]
        o = acc[0:HEAD_DIM, :] / acc[HEAD_DIM:HEAD_DIM + 1, :]
        for jp in range(2):
            blk = jnp.concatenate([o[:, (2 * jp) * SEL_TQ:(2 * jp + 1) * SEL_TQ],
                                   o[:, (2 * jp + 1) * SEL_TQ:(2 * jp + 2) * SEL_TQ]], axis=0)
            o_ref[:, jp * LANES:(jp + 1) * LANES] = blk.T


def _sel_attn(qkv, qt, vt, mq, bias_s, rel_bias, seq):
    n_q = seq // SEL_TQ
    per_step = SEL_TK // SEL_TQ
    qi_l, kt_l = [], []
    for qi in range(n_q):
        for kt in range(qi // per_step + 1):
            qi_l.append(qi)
            kt_l.append(kt)
    qi_arr = jnp.asarray(np.array(qi_l, np.int32))
    kt_arr = jnp.asarray(np.array(kt_l, np.int32))
    win_steps = LANES * SEL_BLOCK // SEL_TK
    n_e = min(LANES * SEL_BLOCK // SEL_SUB, seq // SEL_SUB)
    e_np = np.zeros((n_e, SEL_SUB, LANES), np.float32)
    for t in range(n_e):
        for k in range(SEL_SUB):
            e_np[t, k, (t * (SEL_SUB // SEL_BLOCK) + k // SEL_BLOCK) % LANES] = 1.0
    e_all = jnp.asarray(e_np, dtype=BF16)
    b31 = rel_bias[REL_BUCKETS - 1, 4:12]
    grid_spec = pltpu.PrefetchScalarGridSpec(
        num_scalar_prefetch=2,
        grid=(2, len(qi_l)),
        in_specs=[
            pl.BlockSpec((4 * HEAD_DIM, SEL_TQ), lambda h, s, qi, kt: (h, qi[s])),
            pl.BlockSpec((None, None, LANES, SEL_TQ), lambda h, s, qi, kt: (h, qi[s], kt[s] // win_steps, 0)),
            pl.BlockSpec((SEL_TK, LANES), lambda h, s, qi, kt: (kt[s], _qkv_col(COL_CKS) // LANES)),
            pl.BlockSpec((None, LANES, SEL_TK), lambda h, s, qi, kt: (h, 0, kt[s])),
            pl.BlockSpec((n_e, SEL_SUB, LANES), lambda h, s, qi, kt: (0, 0, 0)),
            pl.BlockSpec((4, SEL_NSLAB, BLOCK, BLOCK), lambda h, s, qi, kt: (h, 0, 0, 0)),
            pl.BlockSpec(memory_space=pltpu.SMEM),
        ],
        out_specs=pl.BlockSpec((SEL_TQ, 256), lambda h, s, qi, kt: (qi[s], h)),
        scratch_shapes=[pltpu.VMEM((2 * LANES, 4 * SEL_TQ), BF16),
                        pltpu.VMEM((1, 4 * SEL_TQ), F32),
                        pltpu.VMEM((SEL_VROWS, 4 * SEL_TQ), F32),
                        pltpu.VMEM((SEL_TK, 4 * SEL_TQ), F32)],
    )
    return pl.pallas_call(
        functools.partial(_sel_kernel, n_e=n_e),
        out_shape=jax.ShapeDtypeStruct((seq, 512), F32),
        grid_spec=grid_spec,
        compiler_params=_cparams(("arbitrary", "arbitrary")),
        name="selected_attn",
    )(qi_arr, kt_arr, qt, mq, qkv, vt, e_all, bias_s, b31)


def _layer_norm(z, g, b):
    mu = jnp.mean(z, axis=-1, keepdims=True)
    zc = z - mu
    var = jnp.mean(jnp.square(zc), axis=-1, keepdims=True)
    return zc * lax.rsqrt(var + LN_EPS) * g + b


def _outproj_kernel(x_ref, oa0, oa1, oa2, la0, la1, la2, ob_ref, ocmp_ref, oslc_ref, owin_ref, g_ref,
                    gx_ref, w_ref, ga_ref, lng_ref, lnb_ref, o_ref):
    l0, l1, l2 = la0[...], la1[...], la2[...]
    mx = jnp.maximum(jnp.maximum(l0, l1), l2)
    w0, w1, w2 = jnp.exp(l0 - mx), jnp.exp(l1 - mx), jnp.exp(l2 - mx)
    oa = (w0 * oa0[...] + w1 * oa1[...] + w2 * oa2[...]) / (w0 + w1 + w2)
    gt = jax.nn.sigmoid(g_ref[...])
    hi = gt.astype(BF16)
    lo = (gt - hi.astype(F32)).astype(BF16)
    gx = gx_ref[...]
    gates = _dot(hi, gx) + _dot(lo, gx)
    oc = gates[:, 0:512] * ocmp_ref[...] + gates[:, 512:1024] * oslc_ref[...] + gates[:, 1024:1536] * owin_ref[...]
    mixed = jnp.concatenate([oa, ob_ref[...], oc], axis=1).astype(BF16)
    y = _dot(mixed, w_ref[...])
    z = DN_ALPHA * x_ref[...] + (1 + ga_ref[...]) * y
    o_ref[...] = _layer_norm(z, lng_ref[...], lnb_ref[...])


def _gate_expand():
    gx = np.zeros((LANES, 3 * 512), np.float32)
    for h in range(8):
        for b in range(3):
            gx[h * 3 + b, b * 512 + h * HEAD_DIM:b * 512 + (h + 1) * HEAD_DIM] = 1.0
    return jnp.asarray(gx, dtype=BF16)


def _outproj(x, oa, la, ob, ocmp, oslc, owin, gates, w_out_l, mod_l, lng, lnb, seq):
    tm = 256
    row = lambda w: pl.BlockSpec((tm, w), lambda i: (i, 0))
    vec = lambda k: pl.BlockSpec((1, D_MODEL), lambda i, k=k: (0, k))
    return pl.pallas_call(
        _outproj_kernel,
        out_shape=jax.ShapeDtypeStruct((seq, D_MODEL), F32),
        grid=(seq // tm,),
        in_specs=[row(D_MODEL)] + [row(256)] * 7 + [row(512)] * 3 + [row(LANES)] + [
            pl.BlockSpec((LANES, 3 * 512), lambda i: (0, 0)),
            pl.BlockSpec((D_MODEL, D_MODEL), lambda i: (0, 0)),
            vec(2), vec(0), vec(0)],
        out_specs=row(D_MODEL),
        compiler_params=_cparams(("arbitrary",)),
        name="out_proj_ln",
    )(x, oa[0], oa[1], oa[2], la[0], la[1], la[2], ob, ocmp, oslc, owin, gates, _gate_expand(), w_out_l,
      mod_l, lng, lnb)


MLP_TF = 1024


def _mlp_kernel(x_ref, sc_ref, sh_ref, g_ref, w1_ref, w2_ref, lng_ref, lnb_ref, o_ref):
    x = x_ref[...]
    h = (x * (1 + sc_ref[...]) + sh_ref[...]).astype(BF16)
    acc = None
    for c0 in range(0, D_FF, MLP_TF):
        f = jnp.maximum(_dot(h, w1_ref[:, c0:c0 + MLP_TF]), 0.0)
        part = _dot(jnp.square(f).astype(BF16), w2_ref[c0:c0 + MLP_TF, :])
        acc = part if acc is None else acc + part
    z = DN_ALPHA * x + (1 + g_ref[...]) * acc
    o_ref[...] = _layer_norm(z, lng_ref[...], lnb_ref[...])


def _mlp(x, mod_l, w1_l, w2_l, lng, lnb, seq):
    tm = 512
    vec = lambda k: pl.BlockSpec((1, D_MODEL), lambda i, k=k: (0, k))
    resident = pl.Buffered(1)
    return pl.pallas_call(
        _mlp_kernel,
        out_shape=jax.ShapeDtypeStruct((seq, D_MODEL), F32),
        grid=(seq // tm,),
        in_specs=[
            pl.BlockSpec((tm, D_MODEL), lambda i: (i, 0)),
            vec(4), vec(3), vec(5),
            pl.BlockSpec((D_MODEL, D_FF), lambda i: (0, 0), pipeline_mode=resident),
            pl.BlockSpec((D_FF, D_MODEL), lambda i: (0, 0), pipeline_mode=resident),
            vec(0), vec(0),
        ],
        out_specs=pl.BlockSpec((tm, D_MODEL), lambda i: (i, 0)),
        compiler_params=_cparams(("arbitrary",)),
        name="mlp_ln",
    )(x, mod_l, mod_l, mod_l, w1_l, w2_l, lng, lnb)


def kernel(x, c, positions, w_in, w_out, rel_bias, sinks, cmp_pos, cmp_w1, cmp_w2, mlp_w1, mlp_w2,
           ada_w, ada_b, ln_g, ln_b):
    bsz, seq, d = x.shape
    assert bsz == 1 and d == D_MODEL and seq % (2 * A_CONFIGS[-1][0]) == 0
    x = x.reshape(seq, D_MODEL)

    mod = _modulation(c, ada_w, ada_b)
    cos_t, sin_t = _rope_tables(positions, seq)
    bias_a, bias_w, bias_s = _bias_tables(rel_bias)

    w_in_b = jnp.pad(w_in, ((0, 0), (0, 0), (0, N_IN_PAD - N_IN))).astype(BF16)
    w_out_b = w_out.astype(BF16)
    w1_b = mlp_w1.astype(BF16)
    w2_b = mlp_w2.astype(BF16)
    cw1_b = cmp_w1.astype(BF16)
    cw2_b = cmp_w2.astype(BF16)

    for l in range(DEPTH):
        mod_l = mod[l]
        qkv, ck, gates, qt, vt, a_qkv = _inproj(x, mod_l, w_in_b[l], cos_t, sin_t, seq)

        oa, la = [], []
        for ci, (window, dil) in enumerate(A_CONFIGS):
            nrows = seq // dil
            view = a_qkv.reshape(nrows, dil * N_A)
            o_c, l_c = _banded_call(
                view, nrows, dil, N_A // LANES, 0, 2, 4, 2,
                nprev=1, tq=min(BAND_TQ, nrows), max_dist=window // dil, bias=bias_a[ci], want_lse=True,
                group_div=None, kv_per_qblock=True, name=f"dilated_attn_{dil}")
            oa.append(o_c.reshape(seq, 256))
            la.append(l_c.reshape(seq, 256))

        (ob,) = _banded_call(
            qkv, seq, 1, N_QKV // LANES, _qkv_col(COL_BQ) // LANES, _qkv_col(COL_BK) // LANES,
            _qkv_col(COL_BV) // LANES, 2,
            nprev=1, tq=BAND_TQ, max_dist=B_WINDOW - 1, sinks=sinks[l], group_div=1,
            kv_per_qblock=False, name="swa_attn")

        kc, vct = _compress(ck, cmp_pos[l], cw1_b[l], cw2_b[l], seq)
        ocmp, mq = _cmp_attn(qt, kc, vct, seq)
        oslc = _sel_attn(qkv, qt, vt, mq, bias_s, rel_bias, seq)
        (owin,) = _banded_call(
            qkv, seq, 1, N_QKV // LANES, _qkv_col(COL_CQ) // LANES, _qkv_col(COL_CKW) // LANES,
            _qkv_col(COL_CVW) // LANES, 4,
            nprev=C_WINDOW // BLOCK, tq=BAND_TQ, max_dist=C_WINDOW - 1, bias=bias_w, group_div=2,
            kv_per_qblock=False, name="window_attn")

        x = _outproj(x, oa, la, ob, ocmp, oslc, owin, gates, w_out_b[l], mod_l,
                     ln_g[l, 0:1], ln_b[l, 0:1], seq)
        x = _mlp(x, mod_l, w1_b[l], w2_b[l], ln_g[l, 1:2], ln_b[l, 1:2], seq)

    return x.reshape(bsz, seq, D_MODEL)
```
